```python
import jax, jax.numpy as jnp
from jax import lax
import numpy as np

D_MODEL = 1024
BATCH = 4
SEQ = 8192
DEPTH = 2
DEC_BATCH = 32
DEC_SEQ = 1
PAST_LEN = 16384
PAGE_SIZE = 128

N_HEADS = 16
HEAD_DIM = D_MODEL // N_HEADS
Q_BLOCK = 128
CHUNK = 128
SGU_WIDTH = 2 * D_MODEL
SGU_GROUPS = SGU_WIDTH // 128
SGU_GROUP_DIM = SGU_WIDTH // SGU_GROUPS
D_FF = 2816
N_EXPERTS = 8
TOP_K = 2
D_FF_EXPERT = 3584
RMS_EPS = 1e-6
SCORE_BIAS_NEAR = -4.0
SCORE_BIAS_FAR = -10.0

kernel_name = "sb_attn_chunk_sgu_hybrid_step"


def rmsnorm(x, g):
    xf = x.astype(jnp.float32)
    y = xf * lax.rsqrt(jnp.mean(xf * xf, axis=-1, keepdims=True) + RMS_EPS)
    return (y * g.astype(jnp.float32)).astype(x.dtype)


def ada_mod(c, w, b):
    m = jax.nn.silu(c) @ w + b
    shift, scale, gate = jnp.split(m, 3, axis=-1)
    return shift[:, None], scale[:, None], gate[:, None]


def stick_breaking(q, k, v, b_score, q_pos, k_pos):
    z = (jnp.einsum('bqhd,bkhd->bhqk', q, k).astype(jnp.float32) * (HEAD_DIM ** -0.5)
         + b_score.astype(jnp.float32)[None, :, None, None])
    mask = k_pos[None, :] < q_pos[:, None]
    log_1m = jnp.where(mask, jax.nn.log_sigmoid(-z), 0.0)
    rev = lax.cumsum(log_1m, axis=3, reverse=True)
    log_a = jax.nn.log_sigmoid(z) + (rev - log_1m)
    a = jnp.where(mask, jnp.exp(log_a), 0.0)
    return jnp.einsum('bhqk,bkhd->bqhd', a, v.astype(jnp.float32)).astype(q.dtype)


def sb_blocked(q, k, v, b_score, q_pos, k_pos):
    tq = q.shape[1]
    if tq <= Q_BLOCK:
        return stick_breaking(q, k, v, b_score, q_pos, k_pos)
    b = q.shape[0]
    pad = (-tq) % Q_BLOCK
    qp = jnp.pad(q, ((0, 0), (0, pad), (0, 0), (0, 0)))
    pp = jnp.pad(q_pos, (0, pad), mode='edge')
    nb = (tq + pad) // Q_BLOCK
    qb = qp.reshape(b, nb, Q_BLOCK, N_HEADS, HEAD_DIM).transpose(1, 0, 2, 3, 4)
    pb = pp.reshape(nb, Q_BLOCK)
    ob = lax.map(lambda a: stick_breaking(a[0], k, v, b_score, a[1], k_pos), (qb, pb))
    return ob.transpose(1, 0, 2, 3, 4).reshape(b, tq + pad, N_HEADS, HEAD_DIM)[:, :tq]


def qkv_heads(h, w_qkv):
    b, t, _ = h.shape
    q, k, v = jnp.split(h @ w_qkv, 3, axis=-1)
    shp = (b, t, N_HEADS, HEAD_DIM)
    return q.reshape(shp), k.reshape(shp), v.reshape(shp)


def attn_prompt(h, w_qkv, w_o, b_score):
    b, t, _ = h.shape
    q, k, v = qkv_heads(h, w_qkv)
    pos = jnp.arange(t, dtype=jnp.int32)
    o = sb_blocked(q, k, v, b_score, pos, pos)
    return o.reshape(b, t, D_MODEL) @ w_o, k, v


def attn_sample(h, cache_k, cache_v, page_table, w_qkv, w_o, b_score):
    b, t, _ = h.shape
    q, k, v = qkv_heads(h, w_qkv)
    past = page_table.shape[1] * PAGE_SIZE
    k_past = cache_k[page_table].reshape(b, past, N_HEADS, HEAD_DIM)
    v_past = cache_v[page_table].reshape(b, past, N_HEADS, HEAD_DIM)
    k_all = jnp.concatenate([k_past, k], axis=1)
    v_all = jnp.concatenate([v_past, v], axis=1)
    q_pos = past + jnp.arange(t, dtype=jnp.int32)
    k_pos = jnp.arange(past + t, dtype=jnp.int32)
    o = sb_blocked(q, k_all, v_all, b_score, q_pos, k_pos)
    return o.reshape(b, t, D_MODEL) @ w_o, k, v


def chunk_sgu(h, w_in, g_v, w_s, b_s, w_out):
    b, t, _ = h.shape
    pad = (-t) % CHUNK
    hp = jnp.pad(h, ((0, 0), (0, pad), (0, 0)))
    tp = t + pad
    z = jax.nn.gelu(hp @ w_in)
    u, v = jnp.split(z, 2, axis=-1)
    v = rmsnorm(v, g_v)
    vc = v.reshape(b, tp // CHUNK, CHUNK, SGU_GROUPS, SGU_GROUP_DIM)
    causal = jnp.tril(jnp.ones((CHUNK, CHUNK), dtype=bool))
    ws = jnp.where(causal[None], w_s, 0.0)
    mixed = jnp.einsum('gij,bnjgc->bnigc', ws, vc) + b_s.T[None, None, :, :, None]
    y = (u * mixed.reshape(b, tp, SGU_WIDTH))[:, :t] @ w_out
    return y, v[:, :t]


def swiglu(h, w_gate, w_up, w_down):
    return (jax.nn.silu(h @ w_gate) * (h @ w_up)) @ w_down


def moe_swiglu(h, w_router, we_gate, we_up, we_down):
    shp = h.shape
    t = h.reshape(-1, D_MODEL)
    logits = (t @ w_router).astype(jnp.float32)
    top_v, top_i = lax.top_k(logits, TOP_K)
    gates = jax.nn.softmax(top_v, axis=-1)
    combine = jnp.sum(jax.nn.one_hot(top_i, N_EXPERTS, dtype=jnp.float32) * gates[..., None], axis=1)
    out = jnp.zeros(t.shape, jnp.float32)
    for e in range(N_EXPERTS):
        ye = swiglu(t, we_gate[e], we_up[e], we_down[e]).astype(jnp.float32)
        out = out + combine[:, e:e + 1] * ye
    return out.astype(h.dtype).reshape(shp)


def setup_inputs(seed: int = 0) -> dict:
    key = jax.random.key(seed)
    ks = jax.random.split(key, 32)
    n_pages = PAST_LEN // PAGE_SIZE
    n_used = DEC_BATCH * n_pages
    n_pool = (n_used * 5) // 4

    def nrm(k, shape, scale=1.0):
        return jax.random.normal(k, shape, jnp.float32) * scale

    page_table = jax.random.permutation(ks[0], n_pool)[:n_used].reshape(DEC_BATCH, n_pages).astype(jnp.int32)
    b_score = jnp.linspace(SCORE_BIAS_NEAR, SCORE_BIAS_FAR, N_HEADS, dtype=jnp.float32) + nrm(ks[25], (N_HEADS,), 0.1)
    return {
        'x_prompt': nrm(ks[1], (BATCH, SEQ, D_MODEL)),
        'x_sample': nrm(ks[2], (DEC_BATCH, DEC_SEQ, D_MODEL)),
        'cache_k': nrm(ks[3], (n_pool, PAGE_SIZE, N_HEADS, HEAD_DIM)),
        'cache_v': nrm(ks[4], (n_pool, PAGE_SIZE, N_HEADS, HEAD_DIM)),
        'page_table': page_table,
        'c_prompt': nrm(ks[5], (BATCH, D_MODEL)),
        'c_sample': nrm(ks[6], (DEC_BATCH, D_MODEL)),
        'norm_g': 1.0 + nrm(ks[7], (DEPTH, 2, D_MODEL), 0.02),
        'ada_w': nrm(ks[8], (DEPTH, 2, D_MODEL, 3 * D_MODEL), 0.5 * D_MODEL ** -0.5),
        'ada_b': nrm(ks[9], (DEPTH, 2, 3 * D_MODEL), 0.02),
        'final_g': 1.0 + nrm(ks[10], (D_MODEL,), 0.02),
        'attn_w_qkv': nrm(ks[11], (D_MODEL, 3 * D_MODEL), D_MODEL ** -0.5),
        'attn_w_o': nrm(ks[12], (D_MODEL, D_MODEL), D_MODEL ** -0.5),
        'attn_b_score': b_score,
        'sgu_w_in': nrm(ks[13], (D_MODEL, 2 * SGU_WIDTH), D_MODEL ** -0.5),
        'sgu_g_v': 1.0 + nrm(ks[14], (SGU_WIDTH,), 0.02),
        'sgu_w_s': nrm(ks[15], (SGU_GROUPS, CHUNK, CHUNK), CHUNK ** -0.5),
        'sgu_b_s': 1.0 + nrm(ks[16], (SGU_GROUPS, CHUNK), 0.02),
        'sgu_w_out': nrm(ks[17], (SGU_WIDTH, D_MODEL), SGU_WIDTH ** -0.5),
        'ffn_w_gate': nrm(ks[18], (D_MODEL, D_FF), D_MODEL ** -0.5),
        'ffn_w_up': nrm(ks[19], (D_MODEL, D_FF), D_MODEL ** -0.5),
        'ffn_w_down': nrm(ks[20], (D_FF, D_MODEL), D_FF ** -0.5),
        'moe_w_router': nrm(ks[21], (D_MODEL, N_EXPERTS), D_MODEL ** -0.5),
        'moe_w_gate': nrm(ks[22], (N_EXPERTS, D_MODEL, D_FF_EXPERT), D_MODEL ** -0.5),
        'moe_w_up': nrm(ks[23], (N_EXPERTS, D_MODEL, D_FF_EXPERT), D_MODEL ** -0.5),
        'moe_w_down': nrm(ks[24], (N_EXPERTS, D_FF_EXPERT, D_MODEL), D_FF_EXPERT ** -0.5),
    }


def reference(x_prompt, x_sample, cache_k, cache_v, page_table, c_prompt, c_sample,
              norm_g, ada_w, ada_b, final_g,
              attn_w_qkv, attn_w_o, attn_b_score,
              sgu_w_in, sgu_g_v, sgu_w_s, sgu_b_s, sgu_w_out,
              ffn_w_gate, ffn_w_up, ffn_w_down,
              moe_w_router, moe_w_gate, moe_w_up, moe_w_down):
    xp, xs = x_prompt, x_sample
    for i in range(DEPTH):
        shp, scp, gtp = ada_mod(c_prompt, ada_w[i, 0], ada_b[i, 0])
        shs, scs, gts = ada_mod(c_sample, ada_w[i, 0], ada_b[i, 0])
        hp = rmsnorm(xp, norm_g[i, 0]) * (1.0 + scp) + shp
        hs = rmsnorm(xs, norm_g[i, 0]) * (1.0 + scs) + shs
        if i % 2 == 0:
            mp, k_prompt, v_prompt = attn_prompt(hp, attn_w_qkv, attn_w_o, attn_b_score)
            ms, k_sample, v_sample = attn_sample(hs, cache_k, cache_v, page_table, attn_w_qkv, attn_w_o, attn_b_score)
        else:
            mp, _ = chunk_sgu(hp, sgu_w_in, sgu_g_v, sgu_w_s, sgu_b_s, sgu_w_out)
            ms, sgu_v_sample = chunk_sgu(hs, sgu_w_in, sgu_g_v, sgu_w_s, sgu_b_s, sgu_w_out)
        xp = xp + gtp * mp
        xs = xs + gts * ms
        shp, scp, gtp = ada_mod(c_prompt, ada_w[i, 1], ada_b[i, 1])
        shs, scs, gts = ada_mod(c_sample, ada_w[i, 1], ada_b[i, 1])
        hp = rmsnorm(xp, norm_g[i, 1]) * (1.0 + scp) + shp
        hs = rmsnorm(xs, norm_g[i, 1]) * (1.0 + scs) + shs
        if i % 2 == 0:
            fp = swiglu(hp, ffn_w_gate, ffn_w_up, ffn_w_down)
            fs = swiglu(hs, ffn_w_gate, ffn_w_up, ffn_w_down)
        else:
            fp = moe_swiglu(hp, moe_w_router, moe_w_gate, moe_w_up, moe_w_down)
            fs = moe_swiglu(hs, moe_w_router, moe_w_gate, moe_w_up, moe_w_down)
        xp = xp + gtp * fp
        xs = xs + gts * fs
    y_prompt = rmsnorm(xp, final_g)
    y_sample = rmsnorm(xs, final_g)
    return (y_prompt, y_sample, k_prompt, v_prompt, k_sample, v_sample, sgu_v_sample)
```

```python
import functools

import jax
import jax.numpy as jnp
from jax import lax
from jax.experimental import pallas as pl
from jax.experimental.pallas import tpu as pltpu

F32 = jnp.float32
BF16 = jnp.bfloat16

N_HEADS = 16
HEAD_DIM = 64
CHUNK = 128
SGU_GROUP_DIM = 128
N_EXPERTS = 8
TOP_K = 2
RMS_EPS = 1e-6
PAGE_SIZE = 128

LANES = 128
VMEM_LIMIT = 56 * 1024 * 1024

TM_QKV = 512
TM_FFN = 512
TM_SGU = 256
TM_ROUTE = 512
TQ_ATTN = 512
TK_ATTN = 256
TM_MOE = 512
FC_MOE = 896
TC_COMBINE = 256
PAGES_PER_STEP = 4


def _cparams(sem):
    return pltpu.CompilerParams(dimension_semantics=sem, vmem_limit_bytes=VMEM_LIMIT)


def _resident(shape, index_map):
    return pl.BlockSpec(shape, index_map, pipeline_mode=pl.Buffered(1))


def _dot(a, b):
    return jnp.dot(a, b, preferred_element_type=F32)


def _rms(x, g):
    return x * lax.rsqrt(jnp.mean(x * x, axis=-1, keepdims=True) + RMS_EPS) * g


def _modnorm(x, g, scale, shift):
    return _rms(x, g) * (1.0 + scale) + shift


def _sigmoid(x):
    return 1.0 / (1.0 + jnp.exp(-x))


def _silu(x):
    return x * _sigmoid(x)


def _gelu_tanh(x):
    c = 0.7978845608028654
    return 0.5 * x * (1.0 + jnp.tanh(c * (x + 0.044715 * (x * x * x))))


def _softplus(z):
    return jnp.maximum(z, 0.0) + jnp.log(1.0 + jnp.exp(-jnp.abs(z)))


def _split_bf16(x):
    hi = x.astype(BF16)
    lo = (x - hi.astype(F32)).astype(BF16)
    return hi, lo


def _mod_spec(mod, tm, rows_per_group):
    _, r, d = mod.shape
    tiles_per_group = rows_per_group // tm
    return pl.BlockSpec((1, r, d), lambda i: (i // tiles_per_group, 0, 0))


def _ada_kernel(c_ref, w_ref, b_ref, o_ref):
    s = _silu(c_ref[...])
    o_ref[0] = jnp.dot(s, w_ref[0], preferred_element_type=F32,
                       precision=lax.Precision.HIGHEST) + b_ref[0]


def _ada_modulation(c_all, ada_w, ada_b):
    r, d = c_all.shape
    n_mod = ada_w.shape[0] * ada_w.shape[1]
    w = ada_w.reshape(n_mod, d, 3 * d)
    b = ada_b.reshape(n_mod, 1, 3 * d)
    tn = d
    return pl.pallas_call(
        _ada_kernel,
        grid=(n_mod, 3 * d // tn),
        in_specs=[
            pl.BlockSpec((r, d), lambda i, j: (0, 0)),
            pl.BlockSpec((1, d, tn), lambda i, j: (i, 0, j)),
            pl.BlockSpec((1, 1, tn), lambda i, j: (i, 0, j)),
        ],
        out_specs=pl.BlockSpec((1, r, tn), lambda i, j: (i, 0, j)),
        out_shape=jax.ShapeDtypeStruct((n_mod, r, 3 * d), F32),
        compiler_params=_cparams(("arbitrary", "arbitrary")),
    )(c_all, w, b)


def _qkv_kernel(x_ref, g_ref, sc_ref, sh_ref, w_ref, q_ref, kb_ref, vb_ref, kf_ref, vf_ref):
    d = x_ref.shape[1]
    h = _modnorm(x_ref[...], g_ref[...], sc_ref[0], sh_ref[0]).astype(BF16)
    q = _dot(h, w_ref[:, 0:d])
    q_ref[...] = (q * (HEAD_DIM ** -0.5)).astype(BF16)
    k = _dot(h, w_ref[:, d:2 * d])
    kf_ref[...] = k
    kb_ref[...] = k.astype(BF16)
    v = _dot(h, w_ref[:, 2 * d:3 * d])
    vf_ref[...] = v
    vb_ref[...] = v.astype(BF16)


def _qkv(x, g, scale, shift, w_qkv_b, tm, rows_per_group):
    n, d = x.shape
    row = pl.BlockSpec((tm, d), lambda i: (i, 0))
    return pl.pallas_call(
        _qkv_kernel,
        grid=(n // tm,),
        in_specs=[
            row,
            _resident((1, d), lambda i: (0, 0)),
            _mod_spec(scale, tm, rows_per_group),
            _mod_spec(shift, tm, rows_per_group),
            _resident((d, 3 * d), lambda i: (0, 0)),
        ],
        out_specs=[row, row, row, row, row],
        out_shape=[
            jax.ShapeDtypeStruct((n, d), BF16),
            jax.ShapeDtypeStruct((n, d), BF16),
            jax.ShapeDtypeStruct((n, d), BF16),
            jax.ShapeDtypeStruct((n, d), F32),
            jax.ShapeDtypeStruct((n, d), F32),
        ],
        compiler_params=_cparams(("arbitrary",)),
    )(x, g, scale, shift, w_qkv_b)


def _sb_tile(qh, kblk, vh, bias, upper, carry, valid):
    z = lax.dot_general(qh, kblk, (((1,), (1,)), ((), ())),
                        preferred_element_type=F32) + bias
    sp = _softplus(z)
    lm = -sp
    if valid is not None:
        lm = jnp.where(valid, lm, 0.0)
    hi, lo = _split_bf16(lm)
    rev = _dot(hi, upper) + _dot(lo, upper)
    a = jnp.exp((z - sp) + rev + carry)
    if valid is not None:
        a = jnp.where(valid, a, 0.0)
    pv = _dot(a.astype(BF16), vh)
    new_carry = carry + rev[:, 0:1] + lm[:, 0:1]
    return pv, new_carry


def _attn_kernel(bias_ref, q_ref, k_ref, v_ref, u_ref, o_ref, acc_ref, carry_ref):
    tq = q_ref.shape[1]
    tk = u_ref.shape[0]
    n_sub = tq // tk
    hp = pl.program_id(1)
    qi = pl.program_id(2)

    lane = lax.broadcasted_iota(jnp.int32, (1, LANES), 1)
    first = (lane < HEAD_DIM).astype(F32)
    head_masks = (first.astype(BF16), (1.0 - first).astype(BF16))
    q = q_ref[0]
    qhs = [q * m for m in head_masks]
    biases = [bias_ref[2 * hp], bias_ref[2 * hp + 1]]
    upper = u_ref[...]

    acc_ref[...] = jnp.zeros_like(acc_ref)
    carry_ref[...] = jnp.zeros_like(carry_ref)

    def tile(kstart, valid):
        kblk = k_ref[0, pl.ds(kstart, tk), :]
        vblk = v_ref[0, pl.ds(kstart, tk), :]
        total = None
        for hh in range(2):
            vh = vblk * head_masks[hh]
            pv, new_carry = _sb_tile(qhs[hh], kblk, vh, biases[hh], upper,
                                     carry_ref[hh], valid)
            carry_ref[hh] = new_carry
            total = pv if total is None else total + pv
        acc_ref[...] += total

    row = lax.broadcasted_iota(jnp.int32, (tq, tk), 0)
    col = lax.broadcasted_iota(jnp.int32, (tq, tk), 1)
    for jj in reversed(range(n_sub)):
        kstart = pl.multiple_of(qi * tq + jj * tk, tk)
        tile(kstart, (col + jj * tk) < row)

    n_off = qi * n_sub

    def body(j, c):
        kstart = pl.multiple_of((n_off - 1 - j) * tk, tk)
        tile(kstart, None)
        return c

    lax.fori_loop(0, n_off, body, 0)
    o_ref[0] = acc_ref[...].astype(o_ref.dtype)


def _sb_attention(q, k, v, b_score, upper, tq):
    b, t, d = q.shape
    n_pairs = d // LANES
    return pl.pallas_call(
        _attn_kernel,
        grid=(b, n_pairs, t // tq),
        in_specs=[
            pl.BlockSpec(memory_space=pltpu.SMEM),
            pl.BlockSpec((1, tq, LANES), lambda bi, hp, qi: (bi, qi, hp)),
            pl.BlockSpec((1, t, LANES), lambda bi, hp, qi: (bi, 0, hp)),
            pl.BlockSpec((1, t, LANES), lambda bi, hp, qi: (bi, 0, hp)),
            _resident(upper.shape, lambda bi, hp, qi: (0, 0)),
        ],
        out_specs=pl.BlockSpec((1, tq, LANES), lambda bi, hp, qi: (bi, qi, hp)),
        out_shape=jax.ShapeDtypeStruct((b, t, d), BF16),
        scratch_shapes=[
            pltpu.VMEM((tq, LANES), F32),
            pltpu.VMEM((2, tq, 1), F32),
        ],
        compiler_params=_cparams(("arbitrary", "arbitrary", "arbitrary")),
    )(b_score, q, k, v, upper)


def _decode_attn_kernel(pt_ref, bias_ref, q_ref, kn_ref, vn_ref, u_ref, *rest):
    npg = PAGES_PER_STEP
    k_refs = rest[0:npg]
    v_refs = rest[npg:2 * npg]
    o_ref = rest[2 * npg]
    acc_ref, carry_ref, qrow_ref = rest[2 * npg + 1:]
    p = pl.program_id(1)
    d = q_ref.shape[2]

    head_of_lane = lax.broadcasted_iota(jnp.int32, (N_HEADS, d), 1) // HEAD_DIM
    head_of_row = lax.broadcasted_iota(jnp.int32, (N_HEADS, d), 0)
    own = head_of_lane == head_of_row
    upper = u_ref[...]
    bias = bias_ref[...]

    def visit(kpage, vpage, valid):
        z = lax.dot_general(qrow_ref[...], kpage, (((1,), (1,)), ((), ())),
                            preferred_element_type=F32) + bias
        sp = _softplus(z)
        lm = -sp
        if valid is not None:
            lm = jnp.where(valid, lm, 0.0)
        hi, lo = _split_bf16(lm)
        rev = _dot(hi, upper) + _dot(lo, upper)
        a = jnp.exp((z - sp) + rev + carry_ref[...])
        if valid is not None:
            a = jnp.where(valid, a, 0.0)
        acc_ref[...] += _dot(a.astype(BF16), vpage)
        carry_ref[...] += rev[:, 0:1] + lm[:, 0:1]

    @pl.when(p == 0)
    def _():
        acc_ref[...] = jnp.zeros_like(acc_ref)
        carry_ref[...] = jnp.zeros_like(carry_ref)
        qb = jnp.broadcast_to(q_ref[0].astype(F32), (N_HEADS, d))
        qrow_ref[...] = jnp.where(own, qb, 0.0).astype(BF16)
        n_new = kn_ref.shape[1]
        q_index = 0
        key_pos = lax.broadcasted_iota(jnp.int32, (N_HEADS, PAGE_SIZE), 1)
        valid = (key_pos < n_new) & (key_pos < q_index)
        kpage = jnp.broadcast_to(kn_ref[0], (PAGE_SIZE, d))
        vpage = jnp.broadcast_to(vn_ref[0], (PAGE_SIZE, d))
        visit(kpage, vpage, valid)

    for i in range(npg):
        visit(k_refs[i][0].astype(BF16), v_refs[i][0].astype(BF16), None)

    @pl.when(p == pl.num_programs(1) - 1)
    def _():
        o = jnp.where(own, acc_ref[...], 0.0)
        o_ref[0] = jnp.sum(o, axis=0, keepdims=True).astype(o_ref.dtype)


def _decode_attention(q, k_new, v_new, cache_k, cache_v, page_table, b_score, upper):
    s, n_new, d = k_new.shape
    n_pages = page_table.shape[1]
    npg = PAGES_PER_STEP
    steps = n_pages // npg
    pt_flat = page_table.reshape(-1)

    def page_spec(i):
        def index_map(si, p, pt):
            return (pt[si * n_pages + (n_pages - 1 - (p * npg + i))], 0, 0)
        return pl.BlockSpec((1, PAGE_SIZE, d), index_map)

    seq = lambda si, p, pt: (si, 0, 0)
    fixed = lambda si, p, pt: (0, 0)
    grid_spec = pltpu.PrefetchScalarGridSpec(
        num_scalar_prefetch=1,
        grid=(s, steps),
        in_specs=[
            pl.BlockSpec((N_HEADS, 1), fixed),
            pl.BlockSpec((1, 1, d), seq),
            pl.BlockSpec((1, n_new, d), seq),
            pl.BlockSpec((1, n_new, d), seq),
            pl.BlockSpec(upper.shape, fixed),
        ] + [page_spec(i) for i in range(npg)] + [page_spec(i) for i in range(npg)],
        out_specs=pl.BlockSpec((1, 1, d), seq),
        scratch_shapes=[
            pltpu.VMEM((N_HEADS, d), F32),
            pltpu.VMEM((N_HEADS, 1), F32),
            pltpu.VMEM((N_HEADS, d), BF16),
        ],
    )
    return pl.pallas_call(
        _decode_attn_kernel,
        grid_spec=grid_spec,
        out_shape=jax.ShapeDtypeStruct((s, 1, d), BF16),
        compiler_params=_cparams(("arbitrary", "arbitrary")),
    )(pt_flat, b_score.reshape(N_HEADS, 1), q, k_new, v_new, upper,
      *([cache_k] * npg), *([cache_v] * npg))


def _ffn_kernel(o_ref, x_ref, gt0_ref, wo_ref, g_ref, sc_ref, sh_ref, gt1_ref,
                wg_ref, wu_ref, wd_ref, out_ref, *, ff_chunk):
    x1 = x_ref[...] + gt0_ref[0] * _dot(o_ref[...], wo_ref[...])
    h = _modnorm(x1, g_ref[...], sc_ref[0], sh_ref[0]).astype(BF16)
    d_ff = wg_ref.shape[1]
    f = None
    for c0 in range(0, d_ff, ff_chunk):
        hg = _dot(h, wg_ref[:, c0:c0 + ff_chunk])
        hu = _dot(h, wu_ref[:, c0:c0 + ff_chunk])
        part = _dot((_silu(hg) * hu).astype(BF16), wd_ref[c0:c0 + ff_chunk, :])
        f = part if f is None else f + part
    out_ref[...] = x1 + gt1_ref[0] * f


def _attn_out_ffn(o, x, gate0, w_o_b, g, scale, shift, gate1, wg_b, wu_b, wd_b,
                  tm, rows_per_group):
    n, d = x.shape
    d_ff = wg_b.shape[1]
    row = pl.BlockSpec((tm, d), lambda i: (i, 0))
    fixed = lambda i: (0, 0)
    mod = lambda m: _mod_spec(m, tm, rows_per_group)
    return pl.pallas_call(
        functools.partial(_ffn_kernel, ff_chunk=d_ff // 2),
        grid=(n // tm,),
        in_specs=[
            row, row, mod(gate0), _resident((d, d), fixed),
            _resident((1, d), fixed), mod(scale), mod(shift), mod(gate1),
            _resident((d, d_ff), fixed), _resident((d, d_ff), fixed),
            _resident((d_ff, d), fixed),
        ],
        out_specs=row,
        out_shape=jax.ShapeDtypeStruct((n, d), F32),
        compiler_params=_cparams(("arbitrary",)),
    )(o, x, gate0, w_o_b, g, scale, shift, gate1, wg_b, wu_b, wd_b)


def _sgu_uv(x_ref, g_ref, sc_ref, sh_ref, win_ref, gv_ref):
    width = gv_ref.shape[1]
    h = _modnorm(x_ref[...], g_ref[...], sc_ref[0], sh_ref[0]).astype(BF16)
    u = _gelu_tanh(_dot(h, win_ref[:, 0:width]))
    v = _rms(_gelu_tanh(_dot(h, win_ref[:, width:2 * width])), gv_ref[...])
    return u, v


def _sgu_prompt_kernel(x_ref, g_ref, sc_ref, sh_ref, gt_ref, win_ref, gv_ref,
                       ws_ref, bs_ref, wout_ref, out_ref, mixed_ref):
    tm = x_ref.shape[0]
    n_groups = ws_ref.shape[0]
    u, v = _sgu_uv(x_ref, g_ref, sc_ref, sh_ref, win_ref, gv_ref)
    vb = v.astype(BF16)
    i_pos = lax.broadcasted_iota(jnp.int32, (CHUNK, CHUNK), 0)
    j_pos = lax.broadcasted_iota(jnp.int32, (CHUNK, CHUNK), 1)
    causal = j_pos <= i_pos
    for gi in range(n_groups):
        ws = jnp.where(causal, ws_ref[gi], 0.0).astype(BF16)
        c0 = gi * SGU_GROUP_DIM
        for ci in range(tm // CHUNK):
            r0 = ci * CHUNK
            mixed_ref[r0:r0 + CHUNK, c0:c0 + SGU_GROUP_DIM] = _dot(
                ws, vb[r0:r0 + CHUNK, c0:c0 + SGU_GROUP_DIM])
    bias = bs_ref[...]
    for ci in range(tm // CHUNK):
        r0 = ci * CHUNK
        mixed_ref[r0:r0 + CHUNK, :] += bias
    y = _dot((u * mixed_ref[...]).astype(BF16), wout_ref[...])
    out_ref[...] = x_ref[...] + gt_ref[0] * y


def _sgu_prompt(x, g, scale, shift, gate, win_b, g_v, w_s, bias_rows, wout_b,
                tm, rows_per_group):
    n, d = x.shape
    width = g_v.shape[1]
    row = pl.BlockSpec((tm, d), lambda i: (i, 0))
    fixed = lambda i: (0, 0)
    mod = lambda m: _mod_spec(m, tm, rows_per_group)
    return pl.pallas_call(
        _sgu_prompt_kernel,
        grid=(n // tm,),
        in_specs=[
            row, _resident((1, d), fixed), mod(scale), mod(shift), mod(gate),
            _resident((d, 2 * width), fixed), _resident((1, width), fixed),
            _resident(w_s.shape, lambda i: (0, 0, 0)),
            _resident(bias_rows.shape, fixed), _resident((width, d), fixed),
        ],
        out_specs=row,
        out_shape=jax.ShapeDtypeStruct((n, d), F32),
        scratch_shapes=[pltpu.VMEM((tm, width), F32)],
        compiler_params=_cparams(("arbitrary",)),
    )(x, g, scale, shift, gate, win_b, g_v, w_s, bias_rows, wout_b)


def _sgu_decode_kernel(x_ref, g_ref, sc_ref, sh_ref, gt_ref, win_ref, gv_ref,
                       ws0_ref, bs0_ref, wout_ref, out_ref, v_ref):
    u, v = _sgu_uv(x_ref, g_ref, sc_ref, sh_ref, win_ref, gv_ref)
    v_ref[...] = v
    mixed = ws0_ref[...] * v + bs0_ref[...]
    y = _dot((u * mixed).astype(BF16), wout_ref[...])
    out_ref[...] = x_ref[...] + gt_ref[0] * y


def _sgu_decode(x, g, scale, shift, gate, win_b, g_v, ws0_row, bs0_row, wout_b):
    n, d = x.shape
    width = g_v.shape[1]
    whole = lambda a: pl.BlockSpec(a.shape, lambda i: (0,) * a.ndim)
    args = (x, g, scale, shift, gate, win_b, g_v, ws0_row, bs0_row, wout_b)
    return pl.pallas_call(
        _sgu_decode_kernel,
        grid=(1,),
        in_specs=[whole(a) for a in args],
        out_specs=[pl.BlockSpec((n, d), lambda i: (0, 0)),
                   pl.BlockSpec((n, width), lambda i: (0, 0))],
        out_shape=[jax.ShapeDtypeStruct((n, d), F32),
                   jax.ShapeDtypeStruct((n, width), F32)],
        compiler_params=_cparams(("arbitrary",)),
    )(*args)


def _route_kernel(x_ref, g_ref, sc_ref, sh_ref, wr_ref, h_ref, info_ref):
    h = _modnorm(x_ref[...], g_ref[...], sc_ref[0], sh_ref[0])
    h_ref[...] = h
    logits = jnp.dot(h, wr_ref[...], preferred_element_type=F32,
                     precision=lax.Precision.HIGHEST)
    lane = lax.broadcasted_iota(jnp.int32, logits.shape, 1)
    lane_f = lane.astype(F32)
    neg = jnp.float32(-jnp.inf)
    l1 = jnp.where(lane < N_EXPERTS, logits, neg)
    m1 = jnp.max(l1, axis=-1, keepdims=True)
    i1 = jnp.min(jnp.where(l1 == m1, lane_f, float(LANES)), axis=-1, keepdims=True)
    l2 = jnp.where(lane_f == i1, neg, l1)
    m2 = jnp.max(l2, axis=-1, keepdims=True)
    i2 = jnp.min(jnp.where(l2 == m2, lane_f, float(LANES)), axis=-1, keepdims=True)
    e = jnp.exp(m2 - m1)
    g1 = 1.0 / (1.0 + e)
    g2 = e / (1.0 + e)
    info = jnp.where(lane == 0, g1, 0.0)
    info = jnp.where(lane == 1, g2, info)
    info = jnp.where(lane == 2, i1, info)
    info = jnp.where(lane == 3, i2, info)
    info_ref[...] = info


def _route(x, g, scale, shift, w_router_pad, tm, rows_per_group):
    n, d = x.shape
    row = pl.BlockSpec((tm, d), lambda i: (i, 0))
    fixed = lambda i: (0, 0)
    return pl.pallas_call(
        _route_kernel,
        grid=(n // tm,),
        in_specs=[
            row, _resident((1, d), fixed),
            _mod_spec(scale, tm, rows_per_group), _mod_spec(shift, tm, rows_per_group),
            _resident((d, LANES), fixed),
        ],
        out_specs=[row, pl.BlockSpec((tm, LANES), lambda i: (i, 0))],
        out_shape=[jax.ShapeDtypeStruct((n, d), F32),
                   jax.ShapeDtypeStruct((n, LANES), F32)],
        compiler_params=_cparams(("arbitrary",)),
    )(x, g, scale, shift, w_router_pad)


def _gather_rows_kernel(idx_ref, src_ref, out_ref, sem):
    tm = out_ref.shape[0]

    def row_copy(r, src_row):
        return pltpu.make_async_copy(src_ref.at[pl.ds(src_row, 1)],
                                     out_ref.at[pl.ds(r, 1)], sem)

    def issue(r, c):
        row_copy(r, idx_ref[0, 0, r]).start()
        return c

    def drain(r, c):
        row_copy(r, 0).wait()
        return c

    lax.fori_loop(0, tm, issue, 0)
    lax.fori_loop(0, tm, drain, 0)


def _gather_rows(src, row_index, tm):
    n_out = row_index.shape[0]
    d = src.shape[1]
    n_tiles = n_out // tm
    return pl.pallas_call(
        _gather_rows_kernel,
        grid=(n_tiles,),
        in_specs=[
            pl.BlockSpec((1, 1, tm), lambda i: (i, 0, 0), memory_space=pltpu.SMEM),
            pl.BlockSpec(memory_space=pl.ANY),
        ],
        out_specs=pl.BlockSpec((tm, d), lambda i: (i, 0)),
        out_shape=jax.ShapeDtypeStruct((n_out, d), src.dtype),
        scratch_shapes=[pltpu.SemaphoreType.DMA(())],
        compiler_params=_cparams(("arbitrary",)),
    )(row_index.reshape(n_tiles, 1, tm), src)


def _expert_kernel(te_ref, nu_ref, xs_ref, wg_ref, wu_ref, wd_ref, out_ref,
                   xb_ref, acc_ref):
    i = pl.program_id(0)
    c = pl.program_id(1)

    @pl.when(i < nu_ref[0])
    def _():
        @pl.when(c == 0)
        def _():
            xb_ref[...] = xs_ref[...].astype(BF16)
            acc_ref[...] = jnp.zeros_like(acc_ref)

        xb = xb_ref[...]
        hg = _dot(xb, wg_ref[0])
        hu = _dot(xb, wu_ref[0])
        acc_ref[...] += _dot((_silu(hg) * hu).astype(BF16), wd_ref[0])

        @pl.when(c == pl.num_programs(1) - 1)
        def _():
            out_ref[...] = acc_ref[...]

    @pl.when((i >= nu_ref[0]) & (c == pl.num_programs(1) - 1))
    def _():
        out_ref[...] = jnp.zeros_like(out_ref)


def _expert_swiglu(xs, tile_expert, n_used, wg_b, wu_b, wd_b, tm, fc):
    n_rows, d = xs.shape
    d_ff = wg_b.shape[2]
    n_tiles = n_rows // tm
    n_chunks = d_ff // fc

    def live(i, nu):
        return jnp.minimum(i, nu[0] - 1)

    def chunk(i, c, nu):
        return jnp.where(i < nu[0], c, n_chunks - 1)

    grid_spec = pltpu.PrefetchScalarGridSpec(
        num_scalar_prefetch=2,
        grid=(n_tiles, n_chunks),
        in_specs=[
            pl.BlockSpec((tm, d), lambda i, c, te, nu: (live(i, nu), 0)),
            pl.BlockSpec((1, d, fc), lambda i, c, te, nu: (te[live(i, nu)], 0, chunk(i, c, nu))),
            pl.BlockSpec((1, d, fc), lambda i, c, te, nu: (te[live(i, nu)], 0, chunk(i, c, nu))),
            pl.BlockSpec((1, fc, d), lambda i, c, te, nu: (te[live(i, nu)], chunk(i, c, nu), 0)),
        ],
        out_specs=pl.BlockSpec((tm, d), lambda i, c, te, nu: (i, 0)),
        scratch_shapes=[pltpu.VMEM((tm, d), BF16), pltpu.VMEM((tm, d), F32)],
    )
    return pl.pallas_call(
        _expert_kernel,
        grid_spec=grid_spec,
        out_shape=jax.ShapeDtypeStruct((n_rows, d), F32),
        compiler_params=_cparams(("arbitrary", "arbitrary")),
    )(tile_expert, n_used, xs, wg_b, wu_b, wd_b)


def _combine_kernel(pos_ref, x_ref, info_ref, gt_ref, gf_ref, y_ref, out_ref, ybuf, sem):
    tc = x_ref.shape[0]

    def row_copy(k, r, src_row):
        return pltpu.make_async_copy(y_ref.at[pl.ds(src_row, 1)],
                                     ybuf.at[k, pl.ds(r, 1)], sem)

    def issue(r, c):
        for k in range(TOP_K):
            row_copy(k, r, pos_ref[0, 0, TOP_K * r + k]).start()
        return c

    def drain(r, c):
        for k in range(TOP_K):
            row_copy(k, r, 0).wait()
        return c

    lax.fori_loop(0, tc, issue, 0)
    lax.fori_loop(0, tc, drain, 0)
    info = info_ref[...]
    moe = info[:, 0:1] * ybuf[0] + info[:, 1:2] * ybuf[1]
    x = x_ref[...] + gt_ref[0] * moe
    out_ref[...] = _rms(x, gf_ref[...])


def _combine(x, info, gate, final_g, y_sorted, pos, tc, rows_per_group):
    n, d = x.shape
    n_tiles = n // tc
    return pl.pallas_call(
        _combine_kernel,
        grid=(n_tiles,),
        in_specs=[
            pl.BlockSpec((1, 1, TOP_K * tc), lambda i: (i, 0, 0), memory_space=pltpu.SMEM),
            pl.BlockSpec((tc, d), lambda i: (i, 0)),
            pl.BlockSpec((tc, LANES), lambda i: (i, 0)),
            _mod_spec(gate, tc, rows_per_group),
            _resident((1, d), lambda i: (0, 0)),
            pl.BlockSpec(memory_space=pl.ANY),
        ],
        out_specs=pl.BlockSpec((tc, d), lambda i: (i, 0)),
        out_shape=jax.ShapeDtypeStruct((n, d), F32),
        scratch_shapes=[pltpu.VMEM((TOP_K, tc, d), F32), pltpu.SemaphoreType.DMA(())],
        compiler_params=_cparams(("arbitrary",)),
    )(pos.reshape(n_tiles, 1, TOP_K * tc), x, info, gate, final_g, y_sorted)


def _expert_layout(experts, tm):
    n_assign = experts.shape[0]
    n_tiles = n_assign // tm + N_EXPERTS
    onehot = (experts[:, None] == jnp.arange(N_EXPERTS, dtype=jnp.int32)[None, :]).astype(jnp.int32)
    running = jnp.cumsum(onehot, axis=0)
    counts = running[-1]
    rank = jnp.sum(running * onehot, axis=1) - 1
    tiles_per_expert = (counts + tm - 1) // tm
    tile_end = jnp.cumsum(tiles_per_expert)
    row_start = (tile_end - tiles_per_expert) * tm
    pos = row_start[experts] + rank
    n_used = tile_end[-1:]
    tile_ids = jnp.arange(n_tiles, dtype=jnp.int32)
    tile_expert = jnp.minimum(
        jnp.sum((tile_ids[:, None] >= tile_end[None, :]).astype(jnp.int32), axis=1),
        N_EXPERTS - 1).astype(jnp.int32)
    return pos.astype(jnp.int32), tile_expert, n_used.astype(jnp.int32), n_tiles


def kernel(x_prompt, x_sample, cache_k, cache_v, page_table, c_prompt, c_sample, norm_g, ada_w, ada_b, final_g, attn_w_qkv, attn_w_o, attn_b_score, sgu_w_in, sgu_g_v, sgu_w_s, sgu_b_s, sgu_w_out, ffn_w_gate, ffn_w_up, ffn_w_down, moe_w_router, moe_w_gate, moe_w_up, moe_w_down):
    b, t, d = x_prompt.shape
    s, t_new, _ = x_sample.shape
    assert t_new == 1, "the sample group decodes one token per sequence"
    n_p = b * t

    n_c = b + s
    n_c_pad = -(-n_c // 8) * 8
    c_all = jnp.concatenate(
        [c_prompt, c_sample, jnp.zeros((n_c_pad - n_c, d), F32)], axis=0)
    mods = _ada_modulation(c_all, ada_w, ada_b)

    def mod_parts(layer, sub):
        m = mods[2 * layer + sub]
        parts = []
        for j in range(3):
            col = m[:, j * d:(j + 1) * d]
            parts.append((col[0:b].reshape(b, 1, d), col[b:b + s].reshape(1, s, d)))
        return parts

    bf = lambda w: w.astype(BF16)
    g_row = lambda layer, sub: norm_g[layer, sub].reshape(1, d)
    xp = x_prompt.reshape(n_p, d)
    xs = x_sample.reshape(s, d)

    (sh_p, sh_s), (sc_p, sc_s), (gt0_p, gt0_s) = mod_parts(0, 0)
    w_qkv_b = bf(attn_w_qkv)
    q_p, kb_p, vb_p, k_p, v_p = _qkv(xp, g_row(0, 0), sc_p, sh_p, w_qkv_b, TM_QKV, t)
    q_s, kb_s, vb_s, k_s, v_s = _qkv(xs, g_row(0, 0), sc_s, sh_s, w_qkv_b, s, s)

    tk = TK_ATTN
    key_i = jnp.arange(tk, dtype=jnp.int32)
    upper = (key_i[:, None] > key_i[None, :]).astype(BF16)
    o_p = _sb_attention(q_p.reshape(b, t, d), kb_p.reshape(b, t, d),
                        vb_p.reshape(b, t, d), attn_b_score, upper, TQ_ATTN)
    page_i = jnp.arange(PAGE_SIZE, dtype=jnp.int32)
    upper_page = (page_i[:, None] > page_i[None, :]).astype(BF16)
    n_pool = cache_k.shape[0]
    o_s = _decode_attention(
        q_s.reshape(s, 1, d), kb_s.reshape(s, 1, d), vb_s.reshape(s, 1, d),
        cache_k.reshape(n_pool, PAGE_SIZE, d), cache_v.reshape(n_pool, PAGE_SIZE, d),
        page_table, attn_b_score, upper_page)

    (sh_p, sh_s), (sc_p, sc_s), (gt1_p, gt1_s) = mod_parts(0, 1)
    w_o_b, wg_b, wu_b, wd_b = bf(attn_w_o), bf(ffn_w_gate), bf(ffn_w_up), bf(ffn_w_down)
    xp = _attn_out_ffn(o_p.reshape(n_p, d), xp, gt0_p, w_o_b, g_row(0, 1), sc_p, sh_p,
                       gt1_p, wg_b, wu_b, wd_b, TM_FFN, t)
    xs = _attn_out_ffn(o_s.reshape(s, d), xs, gt0_s, w_o_b, g_row(0, 1), sc_s, sh_s,
                       gt1_s, wg_b, wu_b, wd_b, s, s)

    (sh_p, sh_s), (sc_p, sc_s), (gt_p, gt_s) = mod_parts(1, 0)
    width = sgu_g_v.shape[0]
    n_groups = sgu_w_s.shape[0]
    win_b, wout_b = bf(sgu_w_in), bf(sgu_w_out)
    g_v = sgu_g_v.reshape(1, width)
    bias_rows = jnp.broadcast_to(sgu_b_s.T[:, :, None],
                                 (CHUNK, n_groups, SGU_GROUP_DIM)).reshape(CHUNK, width)
    xp = _sgu_prompt(xp, g_row(1, 0), sc_p, sh_p, gt_p, win_b, g_v, sgu_w_s,
                     bias_rows, wout_b, TM_SGU, t)
    ws0_row = jnp.broadcast_to(sgu_w_s[:, 0, 0][:, None],
                               (n_groups, SGU_GROUP_DIM)).reshape(1, width)
    xs, sgu_v_s = _sgu_decode(xs, g_row(1, 0), sc_s, sh_s, gt_s, win_b, g_v,
                              ws0_row, bias_rows[0:1], wout_b)

    (sh_p, sh_s), (sc_p, sc_s), (gt_p, gt_s) = mod_parts(1, 1)
    wr_pad = jnp.concatenate(
        [moe_w_router, jnp.zeros((d, LANES - N_EXPERTS), F32)], axis=1)
    h_p, info_p = _route(xp, g_row(1, 1), sc_p, sh_p, wr_pad, TM_ROUTE, t)
    h_s, info_s = _route(xs, g_row(1, 1), sc_s, sh_s, wr_pad, s, s)

    h_all = jnp.concatenate([h_p, h_s], axis=0)
    experts = jnp.concatenate([info_p[:, 2:2 + TOP_K], info_s[:, 2:2 + TOP_K]],
                              axis=0).astype(jnp.int32).reshape(-1)
    pos, tile_expert, n_used, n_tiles = _expert_layout(experts, TM_MOE)
    token_of_assignment = jnp.arange(experts.shape[0], dtype=jnp.int32) // TOP_K
    row_token = jnp.zeros((n_tiles * TM_MOE,), jnp.int32).at[pos].set(token_of_assignment)
    x_sorted = _gather_rows(h_all, row_token, TM_MOE)
    y_sorted = _expert_swiglu(x_sorted, tile_expert, n_used, bf(moe_w_gate),
                              bf(moe_w_up), bf(moe_w_down), TM_MOE, FC_MOE)

    fg = final_g.reshape(1, d)
    y_p = _combine(xp, info_p, gt_p, fg, y_sorted, pos[:TOP_K * n_p], TC_COMBINE, t)
    y_s = _combine(xs, info_s, gt_s, fg, y_sorted, pos[TOP_K * n_p:], s, s)

    heads = (N_HEADS, HEAD_DIM)
    return (y_p.reshape(b, t, d), y_s.reshape(s, 1, d),
            k_p.reshape(b, t, *heads), v_p.reshape(b, t, *heads),
            k_s.reshape(s, 1, *heads), v_s.reshape(s, 1, *heads),
            sgu_v_s.reshape(s, 1, width))
```

```python
import functools

import jax
import jax.numpy as jnp
from jax import lax
from jax.experimental import pallas as pl
from jax.experimental.pallas import tpu as pltpu

F32 = jnp.float32
BF16 = jnp.bfloat16

N_HEADS = 16
HEAD_DIM = 64
CHUNK = 128
SGU_GROUP_DIM = 128
N_EXPERTS = 8
TOP_K = 2
RMS_EPS = 1e-6
PAGE_SIZE = 128

LANES = 128
VMEM_LIMIT = 56 * 1024 * 1024

TM_QKV = 512
TM_FFN = 512
TM_SGU = 256
TM_ROUTE = 512
TQ_ATTN = 512
TK_ATTN = 256
TM_MOE = 512
FC_MOE = 896
TC_COMBINE = 256
PAGES_PER_STEP = 8


def _cparams(sem):
    return pltpu.CompilerParams(dimension_semantics=sem, vmem_limit_bytes=VMEM_LIMIT)


def _resident(shape, index_map):
    return pl.BlockSpec(shape, index_map, pipeline_mode=pl.Buffered(1))


def _dot(a, b):
    return jnp.dot(a, b, preferred_element_type=F32)


def _rms(x, g):
    return x * lax.rsqrt(jnp.mean(x * x, axis=-1, keepdims=True) + RMS_EPS) * g


def _modnorm(x, g, scale, shift):
    return _rms(x, g) * (1.0 + scale) + shift


def _sigmoid(x):
    return 1.0 / (1.0 + jnp.exp(-x))


def _silu(x):
    return x * _sigmoid(x)


def _gelu_tanh(x):
    c = 0.7978845608028654
    return 0.5 * x * (1.0 + jnp.tanh(c * (x + 0.044715 * (x * x * x))))


LOG2E = 1.4426950408889634
MASKED_LOG = -1e30


def _neg_abs(z):
    bits = lax.bitcast_convert_type(z, jnp.uint32) | jnp.uint32(0x80000000)
    return lax.bitcast_convert_type(bits, F32)


def _mod_spec(mod, tm, rows_per_group):
    _, r, d = mod.shape
    tiles_per_group = rows_per_group // tm
    return pl.BlockSpec((1, r, d), lambda i: (i // tiles_per_group, 0, 0))


def _ada_kernel(c_ref, w_ref, b_ref, o_ref):
    s = _silu(c_ref[...])
    o_ref[0] = jnp.dot(s, w_ref[0], preferred_element_type=F32,
                       precision=lax.Precision.HIGHEST) + b_ref[0]


def _ada_modulation(c_all, ada_w, ada_b):
    r, d = c_all.shape
    n_mod = ada_w.shape[0] * ada_w.shape[1]
    w = ada_w.reshape(n_mod, d, 3 * d)
    b = ada_b.reshape(n_mod, 1, 3 * d)
    tn = d
    return pl.pallas_call(
        _ada_kernel,
        grid=(n_mod, 3 * d // tn),
        in_specs=[
            pl.BlockSpec((r, d), lambda i, j: (0, 0)),
            pl.BlockSpec((1, d, tn), lambda i, j: (i, 0, j)),
            pl.BlockSpec((1, 1, tn), lambda i, j: (i, 0, j)),
        ],
        out_specs=pl.BlockSpec((1, r, tn), lambda i, j: (i, 0, j)),
        out_shape=jax.ShapeDtypeStruct((n_mod, r, 3 * d), F32),
        compiler_params=_cparams(("arbitrary", "arbitrary")),
    )(c_all, w, b)


def _qkv_kernel(x_ref, g_ref, sc_ref, sh_ref, w_ref, q_ref, kb_ref, vb_ref, kf_ref, vf_ref):
    d = x_ref.shape[1]
    h = _modnorm(x_ref[...], g_ref[...], sc_ref[0], sh_ref[0]).astype(BF16)
    q = _dot(h, w_ref[:, 0:d])
    q_ref[...] = (q * (HEAD_DIM ** -0.5 * LOG2E)).astype(BF16)
    k = _dot(h, w_ref[:, d:2 * d])
    kf_ref[...] = k
    kb_ref[...] = k.astype(BF16)
    v = _dot(h, w_ref[:, 2 * d:3 * d])
    vf_ref[...] = v
    vb_ref[...] = v.astype(BF16)


def _qkv(x, g, scale, shift, w_qkv_b, tm, rows_per_group):
    n, d = x.shape
    row = pl.BlockSpec((tm, d), lambda i: (i, 0))
    return pl.pallas_call(
        _qkv_kernel,
        grid=(n // tm,),
        in_specs=[
            row,
            _resident((1, d), lambda i: (0, 0)),
            _mod_spec(scale, tm, rows_per_group),
            _mod_spec(shift, tm, rows_per_group),
            _resident((d, 3 * d), lambda i: (0, 0)),
        ],
        out_specs=[row, row, row, row, row],
        out_shape=[
            jax.ShapeDtypeStruct((n, d), BF16),
            jax.ShapeDtypeStruct((n, d), BF16),
            jax.ShapeDtypeStruct((n, d), BF16),
            jax.ShapeDtypeStruct((n, d), F32),
            jax.ShapeDtypeStruct((n, d), F32),
        ],
        compiler_params=_cparams(("arbitrary",)),
    )(x, g, scale, shift, w_qkv_b)


def _softplus2(z):
    return jnp.maximum(z, 0.0) + jnp.log2(1.0 + jnp.exp2(_neg_abs(z)))


def _attn_kernel(bias_ref, q_ref, k_ref, v_ref, u_ref, o_ref,
                 acc_ref, carry_ref, zs_ref, sp_ref, a_ref):
    tq = q_ref.shape[1]
    tk = u_ref.shape[0]
    assert tq == 2 * tk, "the pipeline below is written for two key tiles per query block"
    hp = pl.program_id(1)
    qi = pl.program_id(2)

    lane = lax.broadcasted_iota(jnp.int32, (1, LANES), 1)
    first = (lane < HEAD_DIM).astype(F32)
    head_masks = (first.astype(BF16), (1.0 - first).astype(BF16))
    q = q_ref[0]
    qhs = [q * m for m in head_masks]
    biases = [bias_ref[2 * hp], bias_ref[2 * hp + 1]]
    neg_upper = u_ref[...]

    acc_ref[...] = jnp.zeros_like(acc_ref)
    carry_ref[...] = jnp.zeros_like(carry_ref)

    row = lax.broadcasted_iota(jnp.int32, (tq, tk), 0)
    col = lax.broadcasted_iota(jnp.int32, (tq, tk), 1)
    newest_tile = 2 * qi + 1

    def key_start(item):
        return pl.multiple_of((newest_tile - item) * tk, tk)

    def score(item, slot, diag_offset=None):
        kblk = k_ref[0, pl.ds(key_start(item), tk), :]
        for hh in range(2):
            z = lax.dot_general(qhs[hh], kblk, (((1,), (1,)), ((), ())),
                                preferred_element_type=F32) + biases[hh]
            sp = _softplus2(z)
            zs = z - sp
            if diag_offset is not None:
                valid = (col + diag_offset) < row
                sp = jnp.where(valid, sp, 0.0)
                zs = jnp.where(valid, zs, MASKED_LOG)
            zs_ref[slot, hh] = zs
            sp_ref[slot, hh] = sp.astype(BF16)

    def weights(slot):
        for hh in range(2):
            spb = sp_ref[slot, hh]
            later = _dot(spb, neg_upper)
            c = carry_ref[hh]
            a_ref[slot, hh] = jnp.exp2(zs_ref[slot, hh] + later + c).astype(BF16)
            carry_ref[hh] = c + later[:, 0:1] - spb[:, 0:1].astype(F32)

    def values(item, slot):
        vblk = v_ref[0, pl.ds(key_start(item), tk), :]
        acc_ref[...] += (_dot(a_ref[slot, 0], vblk * head_masks[0])
                         + _dot(a_ref[slot, 1], vblk * head_masks[1]))

    score(0, 0, diag_offset=tk)
    score(1, 1, diag_offset=0)
    weights(0)

    def body(jj, c):
        j = 2 + 2 * jj
        score(j, 0)
        weights(1)
        values(j - 2, 0)
        score(j + 1, 1)
        weights(0)
        values(j - 1, 1)
        return c

    lax.fori_loop(0, qi, body, 0)
    n_tiles = 2 + 2 * qi
    weights(1)
    values(n_tiles - 2, 0)
    values(n_tiles - 1, 1)
    o_ref[0] = acc_ref[...].astype(o_ref.dtype)


def _sb_attention(q, k, v, bias2, neg_upper, tq):
    b, t, d = q.shape
    tk = neg_upper.shape[0]
    n_pairs = d // LANES
    return pl.pallas_call(
        _attn_kernel,
        grid=(b, n_pairs, t // tq),
        in_specs=[
            pl.BlockSpec(memory_space=pltpu.SMEM),
            pl.BlockSpec((1, tq, LANES), lambda bi, hp, qi: (bi, qi, hp)),
            pl.BlockSpec((1, t, LANES), lambda bi, hp, qi: (bi, 0, hp)),
            pl.BlockSpec((1, t, LANES), lambda bi, hp, qi: (bi, 0, hp)),
            _resident(neg_upper.shape, lambda bi, hp, qi: (0, 0)),
        ],
        out_specs=pl.BlockSpec((1, tq, LANES), lambda bi, hp, qi: (bi, qi, hp)),
        out_shape=jax.ShapeDtypeStruct((b, t, d), BF16),
        scratch_shapes=[
            pltpu.VMEM((tq, LANES), F32),
            pltpu.VMEM((2, tq, 1), F32),
            pltpu.VMEM((2, 2, tq, tk), F32),
            pltpu.VMEM((2, 2, tq, tk), BF16),
            pltpu.VMEM((2, 2, tq, tk), BF16),
        ],
        compiler_params=_cparams(("arbitrary", "arbitrary", "arbitrary")),
    )(bias2, q, k, v, neg_upper)


def _decode_attn_kernel(pt_ref, bias_ref, q_ref, kn_ref, vn_ref, u_ref, *rest):
    npg = PAGES_PER_STEP
    k_refs = rest[0:npg]
    v_refs = rest[npg:2 * npg]
    o_ref = rest[2 * npg]
    acc_ref, carry_ref, qblk_ref = rest[2 * npg + 1:]
    p = pl.program_id(1)
    d = q_ref.shape[2]

    neg_upper = u_ref[...]
    bias = bias_ref[...]
    head_of_lane = lax.broadcasted_iota(jnp.int32, (N_HEADS, d), 1) // HEAD_DIM
    own = head_of_lane == lax.broadcasted_iota(jnp.int32, (N_HEADS, d), 0)
    nt = (((1,), (1,)), ((), ()))

    def visit(pages, valid):
        scored = []
        for k_ref, _ in pages:
            z = _dot(qblk_ref[...], k_ref[0].astype(BF16)) + bias
            sp = _softplus2(z)
            zs = z - sp
            if valid is not None:
                sp = jnp.where(valid, sp, 0.0)
                zs = jnp.where(valid, zs, MASKED_LOG)
            scored.append((zs, sp.astype(BF16)))
        laters = [_dot(spb, neg_upper) for _, spb in scored]
        c = carry_ref[...]
        weights = []
        for (zs, spb), later in zip(scored, laters):
            weights.append(jnp.exp2(zs + later + c).astype(BF16))
            c = c + later[:, 0:1] - spb[:, 0:1].astype(F32)
        carry_ref[...] = c
        o = acc_ref[...]
        for (_, v_ref), a in zip(pages, weights):
            o = o + lax.dot_general(a, v_ref[0].astype(BF16), nt, preferred_element_type=F32)
        acc_ref[...] = o

    @pl.when(p == 0)
    def _():
        acc_ref[...] = jnp.zeros_like(acc_ref)
        carry_ref[...] = jnp.zeros_like(carry_ref)
        qb = jnp.broadcast_to(q_ref[0], (N_HEADS, d))
        qblk_ref[...] = jnp.where(own, qb, 0.0).astype(BF16)
        n_new = 1
        q_index = 0
        key_pos = lax.broadcasted_iota(jnp.int32, (N_HEADS, PAGE_SIZE), 1)
        visit([(kn_ref, vn_ref)], (key_pos < n_new) & (key_pos < q_index))

    visit(list(zip(k_refs, v_refs)), None)

    @pl.when(p == pl.num_programs(1) - 1)
    def _():
        o_ref[0] = jnp.sum(jnp.where(own, acc_ref[...], 0.0), axis=0, keepdims=True)


def _decode_attention(q, k_new, v_new, cache_k, cache_v, page_table, bias2, neg_upper):
    s, _, d = q.shape
    n_pages = page_table.shape[1]
    npg = PAGES_PER_STEP
    steps = n_pages // npg
    pt_flat = page_table.reshape(-1)
    page_block = (1, d, PAGE_SIZE)

    def page_spec(i):
        def index_map(si, p, pt):
            return (pt[si * n_pages + (n_pages - 1 - (p * npg + i))], 0, 0)
        return pl.BlockSpec(page_block, index_map)

    seq = lambda si, p, pt: (si, 0, 0)
    fixed = lambda si, p, pt: (0, 0)
    per_seq = pl.BlockSpec((1, 1, d), seq)
    new_page = pl.BlockSpec(page_block, seq)
    grid_spec = pltpu.PrefetchScalarGridSpec(
        num_scalar_prefetch=1,
        grid=(s, steps),
        in_specs=[
            pl.BlockSpec((N_HEADS, 1), fixed),
            per_seq, new_page, new_page,
            pl.BlockSpec(neg_upper.shape, fixed),
        ] + [page_spec(i) for i in range(npg)] + [page_spec(i) for i in range(npg)],
        out_specs=per_seq,
        scratch_shapes=[
            pltpu.VMEM((N_HEADS, d), F32),
            pltpu.VMEM((N_HEADS, 1), F32),
            pltpu.VMEM((N_HEADS, d), BF16),
        ],
    )
    return pl.pallas_call(
        _decode_attn_kernel,
        grid_spec=grid_spec,
        out_shape=jax.ShapeDtypeStruct((s, 1, d), F32),
        compiler_params=_cparams(("arbitrary", "arbitrary")),
    )(pt_flat, bias2.reshape(N_HEADS, 1), q, k_new, v_new, neg_upper,
      *([cache_k] * npg), *([cache_v] * npg))


def _ffn_kernel(o_ref, x_ref, gt0_ref, wo_ref, g_ref, sc_ref, sh_ref, gt1_ref,
                wg_ref, wu_ref, wd_ref, out_ref, *, ff_chunk):
    x1 = x_ref[...] + gt0_ref[0] * _dot(o_ref[...], wo_ref[...])
    h = _modnorm(x1, g_ref[...], sc_ref[0], sh_ref[0]).astype(BF16)
    d_ff = wg_ref.shape[1]
    f = None
    for c0 in range(0, d_ff, ff_chunk):
        hg = _dot(h, wg_ref[:, c0:c0 + ff_chunk])
        hu = _dot(h, wu_ref[:, c0:c0 + ff_chunk])
        part = _dot((_silu(hg) * hu).astype(BF16), wd_ref[c0:c0 + ff_chunk, :])
        f = part if f is None else f + part
    out_ref[...] = x1 + gt1_ref[0] * f


def _attn_out_ffn(o, x, gate0, w_o_b, g, scale, shift, gate1, wg_b, wu_b, wd_b,
                  tm, rows_per_group):
    n, d = x.shape
    d_ff = wg_b.shape[1]
    row = pl.BlockSpec((tm, d), lambda i: (i, 0))
    fixed = lambda i: (0, 0)
    mod = lambda m: _mod_spec(m, tm, rows_per_group)
    return pl.pallas_call(
        functools.partial(_ffn_kernel, ff_chunk=d_ff // 2),
        grid=(n // tm,),
        in_specs=[
            row, row, mod(gate0), _resident((d, d), fixed),
            _resident((1, d), fixed), mod(scale), mod(shift), mod(gate1),
            _resident((d, d_ff), fixed), _resident((d, d_ff), fixed),
            _resident((d_ff, d), fixed),
        ],
        out_specs=row,
        out_shape=jax.ShapeDtypeStruct((n, d), F32),
        compiler_params=_cparams(("arbitrary",)),
    )(o, x, gate0, w_o_b, g, scale, shift, gate1, wg_b, wu_b, wd_b)


def _sgu_uv(x_ref, g_ref, sc_ref, sh_ref, win_ref, gv_ref):
    width = gv_ref.shape[1]
    h = _modnorm(x_ref[...], g_ref[...], sc_ref[0], sh_ref[0]).astype(BF16)
    u = _gelu_tanh(_dot(h, win_ref[:, 0:width]))
    v = _rms(_gelu_tanh(_dot(h, win_ref[:, width:2 * width])), gv_ref[...])
    return u, v


def _sgu_prompt_kernel(x_ref, g_ref, sc_ref, sh_ref, gt_ref, win_ref, gv_ref,
                       ws_ref, bs_ref, wout_ref, out_ref, mixed_ref):
    tm = x_ref.shape[0]
    n_groups = ws_ref.shape[0]
    u, v = _sgu_uv(x_ref, g_ref, sc_ref, sh_ref, win_ref, gv_ref)
    vb = v.astype(BF16)
    i_pos = lax.broadcasted_iota(jnp.int32, (CHUNK, CHUNK), 0)
    j_pos = lax.broadcasted_iota(jnp.int32, (CHUNK, CHUNK), 1)
    causal = j_pos <= i_pos
    for gi in range(n_groups):
        ws = jnp.where(causal, ws_ref[gi], 0.0).astype(BF16)
        c0 = gi * SGU_GROUP_DIM
        for ci in range(tm // CHUNK):
            r0 = ci * CHUNK
            mixed_ref[r0:r0 + CHUNK, c0:c0 + SGU_GROUP_DIM] = _dot(
                ws, vb[r0:r0 + CHUNK, c0:c0 + SGU_GROUP_DIM])
    bias = bs_ref[...]
    for ci in range(tm // CHUNK):
        r0 = ci * CHUNK
        mixed_ref[r0:r0 + CHUNK, :] += bias
    y = _dot((u * mixed_ref[...]).astype(BF16), wout_ref[...])
    out_ref[...] = x_ref[...] + gt_ref[0] * y


def _sgu_prompt(x, g, scale, shift, gate, win_b, g_v, w_s, bias_rows, wout_b,
                tm, rows_per_group):
    n, d = x.shape
    width = g_v.shape[1]
    row = pl.BlockSpec((tm, d), lambda i: (i, 0))
    fixed = lambda i: (0, 0)
    mod = lambda m: _mod_spec(m, tm, rows_per_group)
    return pl.pallas_call(
        _sgu_prompt_kernel,
        grid=(n // tm,),
        in_specs=[
            row, _resident((1, d), fixed), mod(scale), mod(shift), mod(gate),
            _resident((d, 2 * width), fixed), _resident((1, width), fixed),
            _resident(w_s.shape, lambda i: (0, 0, 0)),
            _resident(bias_rows.shape, fixed), _resident((width, d), fixed),
        ],
        out_specs=row,
        out_shape=jax.ShapeDtypeStruct((n, d), F32),
        scratch_shapes=[pltpu.VMEM((tm, width), F32)],
        compiler_params=_cparams(("arbitrary",)),
    )(x, g, scale, shift, gate, win_b, g_v, w_s, bias_rows, wout_b)


def _sgu_decode_kernel(x_ref, g_ref, sc_ref, sh_ref, gt_ref, win_ref, gv_ref,
                       ws0_ref, bs0_ref, wout_ref, out_ref, v_ref):
    u, v = _sgu_uv(x_ref, g_ref, sc_ref, sh_ref, win_ref, gv_ref)
    v_ref[...] = v
    mixed = ws0_ref[...] * v + bs0_ref[...]
    y = _dot((u * mixed).astype(BF16), wout_ref[...])
    out_ref[...] = x_ref[...] + gt_ref[0] * y


def _sgu_decode(x, g, scale, shift, gate, win_b, g_v, ws0_row, bs0_row, wout_b):
    n, d = x.shape
    width = g_v.shape[1]
    whole = lambda a: pl.BlockSpec(a.shape, lambda i: (0,) * a.ndim)
    args = (x, g, scale, shift, gate, win_b, g_v, ws0_row, bs0_row, wout_b)
    return pl.pallas_call(
        _sgu_decode_kernel,
        grid=(1,),
        in_specs=[whole(a) for a in args],
        out_specs=[pl.BlockSpec((n, d), lambda i: (0, 0)),
                   pl.BlockSpec((n, width), lambda i: (0, 0))],
        out_shape=[jax.ShapeDtypeStruct((n, d), F32),
                   jax.ShapeDtypeStruct((n, width), F32)],
        compiler_params=_cparams(("arbitrary",)),
    )(*args)


def _route_kernel(x_ref, g_ref, sc_ref, sh_ref, wr_ref, h_ref, info_ref):
    h = _modnorm(x_ref[...], g_ref[...], sc_ref[0], sh_ref[0])
    h_ref[...] = h
    logits = jnp.dot(h, wr_ref[...], preferred_element_type=F32,
                     precision=lax.Precision.HIGHEST)
    lane = lax.broadcasted_iota(jnp.int32, logits.shape, 1)
    lane_f = lane.astype(F32)
    neg = jnp.float32(-jnp.inf)
    l1 = jnp.where(lane < N_EXPERTS, logits, neg)
    m1 = jnp.max(l1, axis=-1, keepdims=True)
    i1 = jnp.min(jnp.where(l1 == m1, lane_f, float(LANES)), axis=-1, keepdims=True)
    l2 = jnp.where(lane_f == i1, neg, l1)
    m2 = jnp.max(l2, axis=-1, keepdims=True)
    i2 = jnp.min(jnp.where(l2 == m2, lane_f, float(LANES)), axis=-1, keepdims=True)
    e = jnp.exp(m2 - m1)
    g1 = 1.0 / (1.0 + e)
    g2 = e / (1.0 + e)
    info = jnp.where(lane == 0, g1, 0.0)
    info = jnp.where(lane == 1, g2, info)
    info = jnp.where(lane == 2, i1, info)
    info = jnp.where(lane == 3, i2, info)
    info_ref[...] = info


def _route(x, g, scale, shift, w_router_pad, tm, rows_per_group):
    n, d = x.shape
    row = pl.BlockSpec((tm, d), lambda i: (i, 0))
    fixed = lambda i: (0, 0)
    return pl.pallas_call(
        _route_kernel,
        grid=(n // tm,),
        in_specs=[
            row, _resident((1, d), fixed),
            _mod_spec(scale, tm, rows_per_group), _mod_spec(shift, tm, rows_per_group),
            _resident((d, LANES), fixed),
        ],
        out_specs=[row, pl.BlockSpec((tm, LANES), lambda i: (i, 0))],
        out_shape=[jax.ShapeDtypeStruct((n, d), F32),
                   jax.ShapeDtypeStruct((n, LANES), F32)],
        compiler_params=_cparams(("arbitrary",)),
    )(x, g, scale, shift, w_router_pad)


def _scatter_rows_kernel(pos_ref, src_ref, *rest, tm, zero_fill):
    if zero_fill:
        out_ref, zero_ref, sem = rest
    else:
        _, out_ref, sem = rest
    i = pl.program_id(0)
    n_copies = TOP_K * tm

    if zero_fill:
        @pl.when(i == 0)
        def _():
            fill = zero_ref.shape[0]
            zero_ref[...] = jnp.zeros_like(zero_ref)

            def fill_copy(j):
                return pltpu.make_async_copy(
                    zero_ref, out_ref.at[pl.ds(pl.multiple_of(j * fill, fill), fill)], sem)

            def start(j, c):
                fill_copy(j).start()
                return c

            def wait(j, c):
                fill_copy(j).wait()
                return c

            lax.fori_loop(0, out_ref.shape[0] // fill, start, 0)
            lax.fori_loop(0, out_ref.shape[0] // fill, wait, 0)

    base = i * tm

    def issue(r, c):
        for k in range(TOP_K):
            pltpu.make_async_copy(src_ref.at[pl.ds(base + r, 1)],
                                  out_ref.at[pl.ds(pos_ref[0, 0, TOP_K * r + k], 1)],
                                  sem).start()
        return c

    lax.fori_loop(0, tm, issue, 0, unroll=8)
    pltpu.make_async_copy(out_ref.at[pl.ds(0, n_copies)],
                          out_ref.at[pl.ds(n_copies, n_copies)], sem).wait()


def _scatter_rows(src, pos, tm, n_out=None, dst=None):
    n, d = src.shape
    n_tiles = n // tm
    zero_fill = dst is None
    n_out = n_out if zero_fill else dst.shape[0]
    any_spec = pl.BlockSpec(memory_space=pl.ANY)
    pos_spec = pl.BlockSpec((1, 1, TOP_K * tm), lambda i: (i, 0, 0), memory_space=pltpu.SMEM)
    scratch = [pltpu.SemaphoreType.DMA(())]
    if zero_fill:
        scratch = [pltpu.VMEM((TM_MOE, d), src.dtype)] + scratch
    return pl.pallas_call(
        functools.partial(_scatter_rows_kernel, tm=tm, zero_fill=zero_fill),
        grid=(n_tiles,),
        in_specs=[pos_spec, any_spec] + ([] if zero_fill else [any_spec]),
        out_specs=any_spec,
        out_shape=jax.ShapeDtypeStruct((n_out, d), src.dtype),
        scratch_shapes=scratch,
        input_output_aliases={} if zero_fill else {2: 0},
        compiler_params=_cparams(("arbitrary",)),
    )(pos.reshape(n_tiles, 1, TOP_K * tm), src, *([] if zero_fill else [dst]))


def _expert_kernel(te_ref, nu_ref, xs_ref, wg_ref, wu_ref, wd_ref, out_ref,
                   xb_ref, acc_ref):
    i = pl.program_id(0)
    c = pl.program_id(1)

    @pl.when(i < nu_ref[0])
    def _():
        @pl.when(c == 0)
        def _():
            xb_ref[...] = xs_ref[...].astype(BF16)
            acc_ref[...] = jnp.zeros_like(acc_ref)

        xb = xb_ref[...]
        hg = _dot(xb, wg_ref[0])
        hu = _dot(xb, wu_ref[0])
        acc_ref[...] += _dot((_silu(hg) * hu).astype(BF16), wd_ref[0])

        @pl.when(c == pl.num_programs(1) - 1)
        def _():
            out_ref[...] = acc_ref[...]

    @pl.when((i >= nu_ref[0]) & (c == pl.num_programs(1) - 1))
    def _():
        out_ref[...] = jnp.zeros_like(out_ref)


def _expert_swiglu(xs, tile_expert, n_used, wg_b, wu_b, wd_b, tm, fc):
    n_rows, d = xs.shape
    d_ff = wg_b.shape[2]
    n_tiles = n_rows // tm
    n_chunks = d_ff // fc

    def live(i, nu):
        return jnp.minimum(i, nu[0] - 1)

    def chunk(i, c, nu):
        return jnp.where(i < nu[0], c, n_chunks - 1)

    grid_spec = pltpu.PrefetchScalarGridSpec(
        num_scalar_prefetch=2,
        grid=(n_tiles, n_chunks),
        in_specs=[
            pl.BlockSpec((tm, d), lambda i, c, te, nu: (live(i, nu), 0)),
            pl.BlockSpec((1, d, fc), lambda i, c, te, nu: (te[live(i, nu)], 0, chunk(i, c, nu))),
            pl.BlockSpec((1, d, fc), lambda i, c, te, nu: (te[live(i, nu)], 0, chunk(i, c, nu))),
            pl.BlockSpec((1, fc, d), lambda i, c, te, nu: (te[live(i, nu)], chunk(i, c, nu), 0)),
        ],
        out_specs=pl.BlockSpec((tm, d), lambda i, c, te, nu: (i, 0)),
        scratch_shapes=[pltpu.VMEM((tm, d), BF16), pltpu.VMEM((tm, d), F32)],
    )
    return pl.pallas_call(
        _expert_kernel,
        grid_spec=grid_spec,
        out_shape=jax.ShapeDtypeStruct((n_rows, d), F32),
        compiler_params=_cparams(("arbitrary", "arbitrary")),
    )(tile_expert, n_used, xs, wg_b, wu_b, wd_b)


def _combine_kernel(pos_ref, x_ref, info_ref, gt_ref, gf_ref, y_ref, out_ref, ybuf, sem):
    tc = x_ref.shape[0]

    def issue(r, c):
        for k in range(TOP_K):
            pltpu.make_async_copy(y_ref.at[pl.ds(pos_ref[0, 0, TOP_K * r + k], 1)],
                                  ybuf.at[k, pl.ds(r, 1)], sem).start()
        return c

    lax.fori_loop(0, tc, issue, 0, unroll=min(8, tc))
    for k in range(TOP_K):
        pltpu.make_async_copy(y_ref.at[pl.ds(0, tc)], ybuf.at[k], sem).wait()
    info = info_ref[...]
    moe = info[:, 0:1] * ybuf[0] + info[:, 1:2] * ybuf[1]
    x = x_ref[...] + gt_ref[0] * moe
    out_ref[...] = _rms(x, gf_ref[...])


def _combine(x, info, gate, final_g, y_sorted, pos, tc, rows_per_group):
    n, d = x.shape
    n_tiles = n // tc
    return pl.pallas_call(
        _combine_kernel,
        grid=(n_tiles,),
        in_specs=[
            pl.BlockSpec((1, 1, TOP_K * tc), lambda i: (i, 0, 0), memory_space=pltpu.SMEM),
            pl.BlockSpec((tc, d), lambda i: (i, 0)),
            pl.BlockSpec((tc, LANES), lambda i: (i, 0)),
            _mod_spec(gate, tc, rows_per_group),
            _resident((1, d), lambda i: (0, 0)),
            pl.BlockSpec(memory_space=pl.ANY),
        ],
        out_specs=pl.BlockSpec((tc, d), lambda i: (i, 0)),
        out_shape=jax.ShapeDtypeStruct((n, d), F32),
        scratch_shapes=[pltpu.VMEM((TOP_K, tc, d), F32), pltpu.SemaphoreType.DMA(())],
        compiler_params=_cparams(("arbitrary",)),
    )(pos.reshape(n_tiles, 1, TOP_K * tc), x, info, gate, final_g, y_sorted)


def _expert_layout(experts, tm):
    n_assign = experts.shape[0]
    n_tiles = n_assign // tm + N_EXPERTS
    onehot = (experts[:, None] == jnp.arange(N_EXPERTS, dtype=jnp.int32)[None, :]).astype(jnp.int32)
    running = jnp.cumsum(onehot, axis=0)
    counts = running[-1]
    rank = jnp.sum(running * onehot, axis=1) - 1
    tiles_per_expert = (counts + tm - 1) // tm
    tile_end = jnp.cumsum(tiles_per_expert)
    row_start = (tile_end - tiles_per_expert) * tm
    pos = row_start[experts] + rank
    n_used = tile_end[-1:]
    tile_ids = jnp.arange(n_tiles, dtype=jnp.int32)
    tile_expert = jnp.minimum(
        jnp.sum((tile_ids[:, None] >= tile_end[None, :]).astype(jnp.int32), axis=1),
        N_EXPERTS - 1).astype(jnp.int32)
    return pos.astype(jnp.int32), tile_expert, n_used.astype(jnp.int32), n_tiles


def kernel(x_prompt, x_sample, cache_k, cache_v, page_table, c_prompt, c_sample, norm_g, ada_w, ada_b, final_g, attn_w_qkv, attn_w_o, attn_b_score, sgu_w_in, sgu_g_v, sgu_w_s, sgu_b_s, sgu_w_out, ffn_w_gate, ffn_w_up, ffn_w_down, moe_w_router, moe_w_gate, moe_w_up, moe_w_down):
    b, t, d = x_prompt.shape
    s, t_new, _ = x_sample.shape
    assert t_new == 1, "the sample group decodes one token per sequence"
    n_p = b * t

    n_c = b + s
    n_c_pad = -(-n_c // 8) * 8
    c_all = jnp.concatenate(
        [c_prompt, c_sample, jnp.zeros((n_c_pad - n_c, d), F32)], axis=0)
    mods = _ada_modulation(c_all, ada_w, ada_b)

    def mod_parts(layer, sub):
        m = mods[2 * layer + sub]
        parts = []
        for j in range(3):
            col = m[:, j * d:(j + 1) * d]
            parts.append((col[0:b].reshape(b, 1, d), col[b:b + s].reshape(1, s, d)))
        return parts

    bf = lambda w: w.astype(BF16)
    g_row = lambda layer, sub: norm_g[layer, sub].reshape(1, d)
    xp = x_prompt.reshape(n_p, d)
    xs = x_sample.reshape(s, d)

    (sh_p, sh_s), (sc_p, sc_s), (gt0_p, gt0_s) = mod_parts(0, 0)
    w_qkv_b = bf(attn_w_qkv)
    q_p, kb_p, vb_p, k_p, v_p = _qkv(xp, g_row(0, 0), sc_p, sh_p, w_qkv_b, TM_QKV, t)
    q_s, kb_s, vb_s, k_s, v_s = _qkv(xs, g_row(0, 0), sc_s, sh_s, w_qkv_b, s, s)

    def neg_upper(n):
        i = jnp.arange(n, dtype=jnp.int32)
        return -(i[:, None] > i[None, :]).astype(BF16)

    bias2 = attn_b_score * LOG2E
    o_p = _sb_attention(q_p.reshape(b, t, d), kb_p.reshape(b, t, d),
                        vb_p.reshape(b, t, d), bias2, neg_upper(TK_ATTN), TQ_ATTN)
    keys_last = lambda c: c.transpose(0, 2, 3, 1).reshape(c.shape[0], d, PAGE_SIZE)

    def new_page(a):
        return jnp.pad(a[:, :, None], ((0, 0), (0, 0), (0, PAGE_SIZE - 1)))

    o_s = _decode_attention(
        q_s.astype(F32).reshape(s, 1, d), new_page(k_s), new_page(v_s),
        keys_last(cache_k), keys_last(cache_v), page_table, bias2,
        neg_upper(PAGE_SIZE)).astype(BF16)

    (sh_p, sh_s), (sc_p, sc_s), (gt1_p, gt1_s) = mod_parts(0, 1)
    w_o_b, wg_b, wu_b, wd_b = bf(attn_w_o), bf(ffn_w_gate), bf(ffn_w_up), bf(ffn_w_down)
    xp = _attn_out_ffn(o_p.reshape(n_p, d), xp, gt0_p, w_o_b, g_row(0, 1), sc_p, sh_p,
                       gt1_p, wg_b, wu_b, wd_b, TM_FFN, t)
    xs = _attn_out_ffn(o_s.reshape(s, d), xs, gt0_s, w_o_b, g_row(0, 1), sc_s, sh_s,
                       gt1_s, wg_b, wu_b, wd_b, s, s)

    (sh_p, sh_s), (sc_p, sc_s), (gt_p, gt_s) = mod_parts(1, 0)
    width = sgu_g_v.shape[0]
    n_groups = sgu_w_s.shape[0]
    win_b, wout_b = bf(sgu_w_in), bf(sgu_w_out)
    g_v = sgu_g_v.reshape(1, width)
    bias_rows = jnp.broadcast_to(sgu_b_s.T[:, :, None],
                                 (CHUNK, n_groups, SGU_GROUP_DIM)).reshape(CHUNK, width)
    xp = _sgu_prompt(xp, g_row(1, 0), sc_p, sh_p, gt_p, win_b, g_v, sgu_w_s,
                     bias_rows, wout_b, TM_SGU, t)
    ws0_row = jnp.broadcast_to(sgu_w_s[:, 0, 0][:, None],
                               (n_groups, SGU_GROUP_DIM)).reshape(1, width)
    xs, sgu_v_s = _sgu_decode(xs, g_row(1, 0), sc_s, sh_s, gt_s, win_b, g_v,
                              ws0_row, bias_rows[0:1], wout_b)

    (sh_p, sh_s), (sc_p, sc_s), (gt_p, gt_s) = mod_parts(1, 1)
    wr_pad = jnp.concatenate(
        [moe_w_router, jnp.zeros((d, LANES - N_EXPERTS), F32)], axis=1)
    h_p, info_p = _route(xp, g_row(1, 1), sc_p, sh_p, wr_pad, TM_ROUTE, t)
    h_s, info_s = _route(xs, g_row(1, 1), sc_s, sh_s, wr_pad, s, s)

    experts = jnp.concatenate([info_p[:, 2:2 + TOP_K], info_s[:, 2:2 + TOP_K]],
                              axis=0).astype(jnp.int32).reshape(-1)
    pos, tile_expert, n_used, n_tiles = _expert_layout(experts, TM_MOE)
    x_sorted = _scatter_rows(h_p, pos[:TOP_K * n_p], TM_MOE, n_out=n_tiles * TM_MOE)
    x_sorted = _scatter_rows(h_s, pos[TOP_K * n_p:], s, dst=x_sorted)
    y_sorted = _expert_swiglu(x_sorted, tile_expert, n_used, bf(moe_w_gate),
                              bf(moe_w_up), bf(moe_w_down), TM_MOE, FC_MOE)

    fg = final_g.reshape(1, d)
    y_p = _combine(xp, info_p, gt_p, fg, y_sorted, pos[:TOP_K * n_p], TC_COMBINE, t)
    y_s = _combine(xs, info_s, gt_s, fg, y_sorted, pos[TOP_K * n_p:], s, s)

    heads = (N_HEADS, HEAD_DIM)
    return (y_p.reshape(b, t, d), y_s.reshape(s, 1, d),
            k_p.reshape(b, t, *heads), v_p.reshape(b, t, *heads),
            k_s.reshape(s, 1, *heads), v_s.reshape(s, 1, *heads),
            sgu_v_s.reshape(s, 1, width))
```

```python
import functools

import jax
import jax.numpy as jnp
from jax import lax
from jax.experimental import pallas as pl
from jax.experimental.pallas import tpu as pltpu

F32 = jnp.float32
BF16 = jnp.bfloat16

N_HEADS = 16
HEAD_DIM = 64
CHUNK = 128
SGU_GROUP_DIM = 128
N_EXPERTS = 8
TOP_K = 2
RMS_EPS = 1e-6
PAGE_SIZE = 128

LANES = 128
VMEM_LIMIT = 56 * 1024 * 1024

TM_QKV = 512
TM_FFN = 512
TM_SGU = 256
TM_ROUTE = 512
TQ_ATTN = 512
TK_ATTN = 256
TM_MOE = 512
FC_MOE = 1792
TC_COMBINE = 256
PAGES_PER_STEP = 8


def _cparams(sem):
    return pltpu.CompilerParams(dimension_semantics=sem, vmem_limit_bytes=VMEM_LIMIT)


def _resident(shape, index_map):
    return pl.BlockSpec(shape, index_map, pipeline_mode=pl.Buffered(1))


def _dot(a, b):
    return jnp.dot(a, b, preferred_element_type=F32)


def _rms(x, g):
    return x * lax.rsqrt(jnp.mean(x * x, axis=-1, keepdims=True) + RMS_EPS) * g


def _modnorm(x, g, scale, shift):
    return _rms(x, g) * (1.0 + scale) + shift


def _sigmoid(x):
    return 1.0 / (1.0 + jnp.exp(-x))


def _silu(x):
    return x * _sigmoid(x)


def _gelu_tanh(x):
    c = 0.7978845608028654
    return 0.5 * x * (1.0 + jnp.tanh(c * (x + 0.044715 * (x * x * x))))


LOG2E = 1.4426950408889634
MASKED_LOG = -1e30


def _neg_abs(z):
    bits = lax.bitcast_convert_type(z, jnp.uint32) | jnp.uint32(0x80000000)
    return lax.bitcast_convert_type(bits, F32)


def _mod_spec(mod, tm, rows_per_group):
    _, r, d = mod.shape
    tiles_per_group = rows_per_group // tm
    return pl.BlockSpec((1, r, d), lambda i: (i // tiles_per_group, 0, 0))


def _ada_kernel(c_ref, w_ref, b_ref, o_ref):
    s = _silu(c_ref[...])
    o_ref[0] = jnp.dot(s, w_ref[0], preferred_element_type=F32,
                       precision=lax.Precision.HIGHEST) + b_ref[0]


def _ada_modulation(c_all, ada_w, ada_b):
    r, d = c_all.shape
    n_mod = ada_w.shape[0] * ada_w.shape[1]
    w = ada_w.reshape(n_mod, d, 3 * d)
    b = ada_b.reshape(n_mod, 1, 3 * d)
    tn = d
    return pl.pallas_call(
        _ada_kernel,
        grid=(n_mod, 3 * d // tn),
        in_specs=[
            pl.BlockSpec((r, d), lambda i, j: (0, 0)),
            pl.BlockSpec((1, d, tn), lambda i, j: (i, 0, j)),
            pl.BlockSpec((1, 1, tn), lambda i, j: (i, 0, j)),
        ],
        out_specs=pl.BlockSpec((1, r, tn), lambda i, j: (i, 0, j)),
        out_shape=jax.ShapeDtypeStruct((n_mod, r, 3 * d), F32),
        compiler_params=_cparams(("arbitrary", "arbitrary")),
    )(c_all, w, b)


def _qkv_kernel(x_ref, g_ref, sc_ref, sh_ref, w_ref, q_ref, kb_ref, vb_ref, kf_ref, vf_ref,
                *, channels_first):
    d = x_ref.shape[1]
    h = _modnorm(x_ref[...], g_ref[...], sc_ref[0], sh_ref[0]).astype(BF16)
    q = _dot(h, w_ref[:, 0:d])
    q_ref[...] = (q * (HEAD_DIM ** -0.5 * LOG2E)).astype(BF16)
    k = _dot(h, w_ref[:, d:2 * d])
    kb_ref[...] = k.astype(BF16)
    v = _dot(h, w_ref[:, 2 * d:3 * d])
    vb_ref[...] = v.astype(BF16)
    if channels_first:
        kf_ref[0] = k.T
        vf_ref[0] = v.T
    else:
        kf_ref[...] = k
        vf_ref[...] = v


def _qkv(x, g, scale, shift, w_qkv_b, tm, rows_per_group, channels_first):
    n, d = x.shape
    row = pl.BlockSpec((tm, d), lambda i: (i, 0))
    if channels_first:
        tiles = rows_per_group // tm
        kv_spec = pl.BlockSpec((1, d, tm), lambda i: (i // tiles, 0, i % tiles))
        kv_shape = jax.ShapeDtypeStruct((n // rows_per_group, d, rows_per_group), F32)
    else:
        kv_spec, kv_shape = row, jax.ShapeDtypeStruct((n, d), F32)
    return pl.pallas_call(
        functools.partial(_qkv_kernel, channels_first=channels_first),
        grid=(n // tm,),
        in_specs=[
            row,
            _resident((1, d), lambda i: (0, 0)),
            _mod_spec(scale, tm, rows_per_group),
            _mod_spec(shift, tm, rows_per_group),
            _resident((d, 3 * d), lambda i: (0, 0)),
        ],
        out_specs=[row, row, row, kv_spec, kv_spec],
        out_shape=[
            jax.ShapeDtypeStruct((n, d), BF16),
            jax.ShapeDtypeStruct((n, d), BF16),
            jax.ShapeDtypeStruct((n, d), BF16),
            kv_shape,
            kv_shape,
        ],
        compiler_params=_cparams(("arbitrary",)),
    )(x, g, scale, shift, w_qkv_b)


def _softplus2(z):
    return jnp.maximum(z, 0.0) + jnp.log2(1.0 + jnp.exp2(_neg_abs(z)))


def _attn_kernel(bias_ref, q_ref, k_ref, v_ref, u_ref, o_ref,
                 acc_ref, carry_ref, zs_ref, sp_ref, a_ref, vh_ref):
    tq = q_ref.shape[1]
    tk = u_ref.shape[0]
    assert tq == 2 * tk, "the pipeline below is written for two key tiles per query block"
    hp = pl.program_id(1)
    qi = pl.program_id(2)

    lane = lax.broadcasted_iota(jnp.int32, (1, LANES), 1)
    first = (lane < HEAD_DIM).astype(F32)
    head_masks = (first.astype(BF16), (1.0 - first).astype(BF16))
    q = q_ref[0]
    qhs = [q * m for m in head_masks]
    biases = [bias_ref[2 * hp], bias_ref[2 * hp + 1]]
    neg_from_here = u_ref[...]

    acc_ref[...] = jnp.zeros_like(acc_ref)
    carry_ref[...] = jnp.zeros_like(carry_ref)

    row = lax.broadcasted_iota(jnp.int32, (tq, tk), 0)
    col = lax.broadcasted_iota(jnp.int32, (tq, tk), 1)
    newest_tile = 2 * qi + 1

    def key_start(item):
        return pl.multiple_of((newest_tile - item) * tk, tk)

    def score(item, slot, diag_offset=None):
        kblk = k_ref[0, pl.ds(key_start(item), tk), :]
        for hh in range(2):
            z = lax.dot_general(qhs[hh], kblk, (((1,), (1,)), ((), ())),
                                preferred_element_type=F32) + biases[hh]
            sp = _softplus2(z)
            if diag_offset is not None:
                valid = (col + diag_offset) < row
                sp = jnp.where(valid, sp, 0.0)
                z = jnp.where(valid, z, MASKED_LOG)
            zs_ref[slot, hh] = z
            sp_ref[slot, hh] = sp.astype(BF16)

    def weights(slot):
        for hh in range(2):
            from_here = _dot(sp_ref[slot, hh], neg_from_here)
            c = carry_ref[hh]
            a_ref[slot, hh] = jnp.exp2(zs_ref[slot, hh] + from_here + c).astype(BF16)
            carry_ref[hh] = c + from_here[:, 0:1]

    @pl.when(qi == 0)
    def _():
        for hh in range(2):
            vh_ref[hh] = v_ref[0] * head_masks[hh]

    def values(item, slot):
        keys = pl.ds(key_start(item), tk)
        acc_ref[...] += (_dot(a_ref[slot, 0], vh_ref[0, keys, :])
                         + _dot(a_ref[slot, 1], vh_ref[1, keys, :]))

    score(0, 0, diag_offset=tk)
    score(1, 1, diag_offset=0)
    weights(0)

    def body(jj, c):
        j = 2 + 2 * jj
        score(j, 0)
        weights(1)
        values(j - 2, 0)
        score(j + 1, 1)
        weights(0)
        values(j - 1, 1)
        return c

    lax.fori_loop(0, qi, body, 0)
    n_tiles = 2 + 2 * qi
    weights(1)
    values(n_tiles - 2, 0)
    values(n_tiles - 1, 1)
    o_ref[0] = acc_ref[...].astype(o_ref.dtype)


def _sb_attention(q, k, v, bias2, neg_upper, tq):
    b, t, d = q.shape
    tk = neg_upper.shape[0]
    n_pairs = d // LANES
    return pl.pallas_call(
        _attn_kernel,
        grid=(b, n_pairs, t // tq),
        in_specs=[
            pl.BlockSpec(memory_space=pltpu.SMEM),
            pl.BlockSpec((1, tq, LANES), lambda bi, hp, qi: (bi, qi, hp)),
            pl.BlockSpec((1, t, LANES), lambda bi, hp, qi: (bi, 0, hp)),
            pl.BlockSpec((1, t, LANES), lambda bi, hp, qi: (bi, 0, hp)),
            _resident(neg_upper.shape, lambda bi, hp, qi: (0, 0)),
        ],
        out_specs=pl.BlockSpec((1, tq, LANES), lambda bi, hp, qi: (bi, qi, hp)),
        out_shape=jax.ShapeDtypeStruct((b, t, d), BF16),
        scratch_shapes=[
            pltpu.VMEM((tq, LANES), F32),
            pltpu.VMEM((2, tq, 1), F32),
            pltpu.VMEM((2, 2, tq, tk), F32),
            pltpu.VMEM((2, 2, tq, tk), BF16),
            pltpu.VMEM((2, 2, tq, tk), BF16),
            pltpu.VMEM((2, t, LANES), BF16),
        ],
        compiler_params=_cparams(("arbitrary", "arbitrary", "arbitrary")),
    )(bias2, q, k, v, neg_upper)


def _decode_attn_kernel(pt_ref, bias_ref, q_ref, kn_ref, vn_ref, u_ref, *rest):
    npg = PAGES_PER_STEP
    k_refs = rest[0:npg]
    v_refs = rest[npg:2 * npg]
    o_ref = rest[2 * npg]
    acc_ref, carry_ref, qblk_ref = rest[2 * npg + 1:]
    p = pl.program_id(1)
    d = q_ref.shape[2]

    neg_from_here = u_ref[...]
    bias = bias_ref[...]
    head_of_lane = lax.broadcasted_iota(jnp.int32, (N_HEADS, d), 1) // HEAD_DIM
    own = head_of_lane == lax.broadcasted_iota(jnp.int32, (N_HEADS, d), 0)
    nt = (((1,), (1,)), ((), ()))

    def visit(pages, valid):
        scored = []
        for k_ref, _ in pages:
            z = _dot(qblk_ref[...], k_ref[0].astype(BF16)) + bias
            sp = _softplus2(z)
            if valid is not None:
                sp = jnp.where(valid, sp, 0.0)
                z = jnp.where(valid, z, MASKED_LOG)
            scored.append((z, sp.astype(BF16)))
        sums = [_dot(spb, neg_from_here) for _, spb in scored]
        c = carry_ref[...]
        weights = []
        for (z, _), from_here in zip(scored, sums):
            weights.append(jnp.exp2(z + from_here + c).astype(BF16))
            c = c + from_here[:, 0:1]
        carry_ref[...] = c
        o = acc_ref[...]
        for (_, v_ref), a in zip(pages, weights):
            o = o + lax.dot_general(a, v_ref[0].astype(BF16), nt, preferred_element_type=F32)
        acc_ref[...] = o

    @pl.when(p == 0)
    def _():
        acc_ref[...] = jnp.zeros_like(acc_ref)
        carry_ref[...] = jnp.zeros_like(carry_ref)
        qb = jnp.broadcast_to(q_ref[0], (N_HEADS, d))
        qblk_ref[...] = jnp.where(own, qb, 0.0).astype(BF16)
        n_new = 1
        q_index = 0
        key_pos = lax.broadcasted_iota(jnp.int32, (N_HEADS, PAGE_SIZE), 1)
        visit([(kn_ref, vn_ref)], (key_pos < n_new) & (key_pos < q_index))

    visit(list(zip(k_refs, v_refs)), None)

    @pl.when(p == pl.num_programs(1) - 1)
    def _():
        o_ref[0] = jnp.sum(jnp.where(own, acc_ref[...], 0.0), axis=0, keepdims=True)


def _decode_attention(q, k_new, v_new, cache_k, cache_v, page_table, bias2, neg_upper):
    s, _, d = q.shape
    n_pages = page_table.shape[1]
    npg = PAGES_PER_STEP
    steps = n_pages // npg
    pt_flat = page_table.reshape(-1)
    page_block = (1, d, PAGE_SIZE)

    def page_spec(i):
        def index_map(si, p, pt):
            return (pt[si * n_pages + (n_pages - 1 - (p * npg + i))], 0, 0)
        return pl.BlockSpec(page_block, index_map)

    seq = lambda si, p, pt: (si, 0, 0)
    fixed = lambda si, p, pt: (0, 0)
    per_seq = pl.BlockSpec((1, 1, d), seq)
    new_page = pl.BlockSpec(page_block, seq)
    grid_spec = pltpu.PrefetchScalarGridSpec(
        num_scalar_prefetch=1,
        grid=(s, steps),
        in_specs=[
            pl.BlockSpec((N_HEADS, 1), fixed),
            per_seq, new_page, new_page,
            pl.BlockSpec(neg_upper.shape, fixed),
        ] + [page_spec(i) for i in range(npg)] + [page_spec(i) for i in range(npg)],
        out_specs=per_seq,
        scratch_shapes=[
            pltpu.VMEM((N_HEADS, d), F32),
            pltpu.VMEM((N_HEADS, 1), F32),
            pltpu.VMEM((N_HEADS, d), BF16),
        ],
    )
    return pl.pallas_call(
        _decode_attn_kernel,
        grid_spec=grid_spec,
        out_shape=jax.ShapeDtypeStruct((s, 1, d), F32),
        compiler_params=_cparams(("arbitrary", "arbitrary")),
    )(pt_flat, bias2.reshape(N_HEADS, 1), q, k_new, v_new, neg_upper,
      *([cache_k] * npg), *([cache_v] * npg))


def _ffn_kernel(o_ref, x_ref, gt0_ref, wo_ref, g_ref, sc_ref, sh_ref, gt1_ref,
                wg_ref, wu_ref, wd_ref, out_ref, *, ff_chunk):
    x1 = x_ref[...] + gt0_ref[0] * _dot(o_ref[...], wo_ref[...])
    h = _modnorm(x1, g_ref[...], sc_ref[0], sh_ref[0]).astype(BF16)
    d_ff = wg_ref.shape[1]
    f = None
    for c0 in range(0, d_ff, ff_chunk):
        hg = _dot(h, wg_ref[:, c0:c0 + ff_chunk])
        hu = _dot(h, wu_ref[:, c0:c0 + ff_chunk])
        part = _dot((_silu(hg) * hu).astype(BF16), wd_ref[c0:c0 + ff_chunk, :])
        f = part if f is None else f + part
    out_ref[...] = x1 + gt1_ref[0] * f


def _attn_out_ffn(o, x, gate0, w_o_b, g, scale, shift, gate1, wg_b, wu_b, wd_b,
                  tm, rows_per_group):
    n, d = x.shape
    d_ff = wg_b.shape[1]
    row = pl.BlockSpec((tm, d), lambda i: (i, 0))
    fixed = lambda i: (0, 0)
    mod = lambda m: _mod_spec(m, tm, rows_per_group)
    return pl.pallas_call(
        functools.partial(_ffn_kernel, ff_chunk=d_ff // 2),
        grid=(n // tm,),
        in_specs=[
            row, row, mod(gate0), _resident((d, d), fixed),
            _resident((1, d), fixed), mod(scale), mod(shift), mod(gate1),
            _resident((d, d_ff), fixed), _resident((d, d_ff), fixed),
            _resident((d_ff, d), fixed),
        ],
        out_specs=row,
        out_shape=jax.ShapeDtypeStruct((n, d), F32),
        compiler_params=_cparams(("arbitrary",)),
    )(o, x, gate0, w_o_b, g, scale, shift, gate1, wg_b, wu_b, wd_b)


def _sgu_uv(x_ref, g_ref, sc_ref, sh_ref, win_ref, gv_ref):
    width = gv_ref.shape[1]
    h = _modnorm(x_ref[...], g_ref[...], sc_ref[0], sh_ref[0]).astype(BF16)
    u = _gelu_tanh(_dot(h, win_ref[:, 0:width]))
    v = _rms(_gelu_tanh(_dot(h, win_ref[:, width:2 * width])), gv_ref[...])
    return u, v


def _sgu_prompt_kernel(x_ref, g_ref, sc_ref, sh_ref, gt_ref, win_ref, gv_ref,
                       ws_ref, bs_ref, wout_ref, out_ref, mixed_ref):
    tm = x_ref.shape[0]
    n_groups = ws_ref.shape[0]
    u, v = _sgu_uv(x_ref, g_ref, sc_ref, sh_ref, win_ref, gv_ref)
    vb = v.astype(BF16)
    i_pos = lax.broadcasted_iota(jnp.int32, (CHUNK, CHUNK), 0)
    j_pos = lax.broadcasted_iota(jnp.int32, (CHUNK, CHUNK), 1)
    causal = j_pos <= i_pos
    for gi in range(n_groups):
        ws = jnp.where(causal, ws_ref[gi], 0.0).astype(BF16)
        c0 = gi * SGU_GROUP_DIM
        for ci in range(tm // CHUNK):
            r0 = ci * CHUNK
            mixed_ref[r0:r0 + CHUNK, c0:c0 + SGU_GROUP_DIM] = _dot(
                ws, vb[r0:r0 + CHUNK, c0:c0 + SGU_GROUP_DIM])
    bias = bs_ref[...]
    for ci in range(tm // CHUNK):
        r0 = ci * CHUNK
        mixed_ref[r0:r0 + CHUNK, :] += bias
    y = _dot((u * mixed_ref[...]).astype(BF16), wout_ref[...])
    out_ref[...] = x_ref[...] + gt_ref[0] * y


def _sgu_prompt(x, g, scale, shift, gate, win_b, g_v, w_s, bias_rows, wout_b,
                tm, rows_per_group):
    n, d = x.shape
    width = g_v.shape[1]
    row = pl.BlockSpec((tm, d), lambda i: (i, 0))
    fixed = lambda i: (0, 0)
    mod = lambda m: _mod_spec(m, tm, rows_per_group)
    return pl.pallas_call(
        _sgu_prompt_kernel,
        grid=(n // tm,),
        in_specs=[
            row, _resident((1, d), fixed), mod(scale), mod(shift), mod(gate),
            _resident((d, 2 * width), fixed), _resident((1, width), fixed),
            _resident(w_s.shape, lambda i: (0, 0, 0)),
            _resident(bias_rows.shape, fixed), _resident((width, d), fixed),
        ],
        out_specs=row,
        out_shape=jax.ShapeDtypeStruct((n, d), F32),
        scratch_shapes=[pltpu.VMEM((tm, width), F32)],
        compiler_params=_cparams(("arbitrary",)),
    )(x, g, scale, shift, gate, win_b, g_v, w_s, bias_rows, wout_b)


def _sgu_decode_kernel(x_ref, g_ref, sc_ref, sh_ref, gt_ref, win_ref, gv_ref,
                       ws0_ref, bs0_ref, wout_ref, out_ref, v_ref):
    u, v = _sgu_uv(x_ref, g_ref, sc_ref, sh_ref, win_ref, gv_ref)
    v_ref[...] = v
    mixed = ws0_ref[...] * v + bs0_ref[...]
    y = _dot((u * mixed).astype(BF16), wout_ref[...])
    out_ref[...] = x_ref[...] + gt_ref[0] * y


def _sgu_decode(x, g, scale, shift, gate, win_b, g_v, ws0_row, bs0_row, wout_b):
    n, d = x.shape
    width = g_v.shape[1]
    whole = lambda a: pl.BlockSpec(a.shape, lambda i: (0,) * a.ndim)
    args = (x, g, scale, shift, gate, win_b, g_v, ws0_row, bs0_row, wout_b)
    return pl.pallas_call(
        _sgu_decode_kernel,
        grid=(1,),
        in_specs=[whole(a) for a in args],
        out_specs=[pl.BlockSpec((n, d), lambda i: (0, 0)),
                   pl.BlockSpec((n, width), lambda i: (0, 0))],
        out_shape=[jax.ShapeDtypeStruct((n, d), F32),
                   jax.ShapeDtypeStruct((n, width), F32)],
        compiler_params=_cparams(("arbitrary",)),
    )(*args)


def _route_kernel(x_ref, g_ref, sc_ref, sh_ref, wr_ref, h_ref, info_ref):
    h = _modnorm(x_ref[...], g_ref[...], sc_ref[0], sh_ref[0])
    h_ref[...] = h
    logits = jnp.dot(h, wr_ref[...], preferred_element_type=F32,
                     precision=lax.Precision.HIGHEST)
    lane = lax.broadcasted_iota(jnp.int32, logits.shape, 1)
    lane_f = lane.astype(F32)
    neg = jnp.float32(-jnp.inf)
    l1 = jnp.where(lane < N_EXPERTS, logits, neg)
    m1 = jnp.max(l1, axis=-1, keepdims=True)
    i1 = jnp.min(jnp.where(l1 == m1, lane_f, float(LANES)), axis=-1, keepdims=True)
    l2 = jnp.where(lane_f == i1, neg, l1)
    m2 = jnp.max(l2, axis=-1, keepdims=True)
    i2 = jnp.min(jnp.where(l2 == m2, lane_f, float(LANES)), axis=-1, keepdims=True)
    e = jnp.exp(m2 - m1)
    g1 = 1.0 / (1.0 + e)
    g2 = e / (1.0 + e)
    info = jnp.where(lane == 0, g1, 0.0)
    info = jnp.where(lane == 1, g2, info)
    info = jnp.where(lane == 2, i1, info)
    info = jnp.where(lane == 3, i2, info)
    info_ref[...] = info


def _route(x, g, scale, shift, w_router_pad, tm, rows_per_group):
    n, d = x.shape
    row = pl.BlockSpec((tm, d), lambda i: (i, 0))
    fixed = lambda i: (0, 0)
    return pl.pallas_call(
        _route_kernel,
        grid=(n // tm,),
        in_specs=[
            row, _resident((1, d), fixed),
            _mod_spec(scale, tm, rows_per_group), _mod_spec(shift, tm, rows_per_group),
            _resident((d, LANES), fixed),
        ],
        out_specs=[row, pl.BlockSpec((tm, LANES), lambda i: (i, 0))],
        out_shape=[jax.ShapeDtypeStruct((n, d), F32),
                   jax.ShapeDtypeStruct((n, LANES), F32)],
        compiler_params=_cparams(("arbitrary",)),
    )(x, g, scale, shift, w_router_pad)


def _scatter_rows_kernel(pos_ref, src_ref, *rest, tm, zero_fill):
    if zero_fill:
        out_ref, zero_ref, sem = rest
    else:
        _, out_ref, sem = rest
    i = pl.program_id(0)
    n_copies = TOP_K * tm

    if zero_fill:
        @pl.when(i == 0)
        def _():
            fill = zero_ref.shape[0]
            zero_ref[...] = jnp.zeros_like(zero_ref)

            def fill_copy(j):
                return pltpu.make_async_copy(
                    zero_ref, out_ref.at[pl.ds(pl.multiple_of(j * fill, fill), fill)], sem)

            def start(j, c):
                fill_copy(j).start()
                return c

            def wait(j, c):
                fill_copy(j).wait()
                return c

            lax.fori_loop(0, out_ref.shape[0] // fill, start, 0)
            lax.fori_loop(0, out_ref.shape[0] // fill, wait, 0)

    def issue(r, c):
        for k in range(TOP_K):
            pltpu.make_async_copy(src_ref.at[pl.ds(r, 1)],
                                  out_ref.at[pl.ds(pos_ref[0, 0, TOP_K * r + k], 1)],
                                  sem).start()
        return c

    lax.fori_loop(0, tm, issue, 0, unroll=8)
    pltpu.make_async_copy(out_ref.at[pl.ds(0, n_copies)],
                          out_ref.at[pl.ds(n_copies, n_copies)], sem).wait()


def _scatter_rows(src, pos, tm, n_out=None, dst=None):
    n, d = src.shape
    n_tiles = n // tm
    zero_fill = dst is None
    n_out = n_out if zero_fill else dst.shape[0]
    any_spec = pl.BlockSpec(memory_space=pl.ANY)
    pos_spec = pl.BlockSpec((1, 1, TOP_K * tm), lambda i: (i, 0, 0), memory_space=pltpu.SMEM)
    scratch = [pltpu.SemaphoreType.DMA(())]
    if zero_fill:
        scratch = [pltpu.VMEM((TM_MOE, d), src.dtype)] + scratch
    return pl.pallas_call(
        functools.partial(_scatter_rows_kernel, tm=tm, zero_fill=zero_fill),
        grid=(n_tiles,),
        in_specs=[pos_spec, pl.BlockSpec((tm, d), lambda i: (i, 0))]
        + ([] if zero_fill else [any_spec]),
        out_specs=any_spec,
        out_shape=jax.ShapeDtypeStruct((n_out, d), src.dtype),
        scratch_shapes=scratch,
        input_output_aliases={} if zero_fill else {2: 0},
        compiler_params=_cparams(("arbitrary",)),
    )(pos.reshape(n_tiles, 1, TOP_K * tm), src, *([] if zero_fill else [dst]))


def _expert_kernel(te_ref, nu_ref, xs_ref, wg_ref, wu_ref, wd_ref, out_ref,
                   xb_ref, acc_ref):
    i = pl.program_id(0)
    c = pl.program_id(1)

    @pl.when(i < nu_ref[0])
    def _():
        @pl.when(c == 0)
        def _():
            xb_ref[...] = xs_ref[...].astype(BF16)
            acc_ref[...] = jnp.zeros_like(acc_ref)

        xb = xb_ref[...]
        hg = _dot(xb, wg_ref[0])
        hu = _dot(xb, wu_ref[0])
        acc_ref[...] += _dot((_silu(hg) * hu).astype(BF16), wd_ref[0])

        @pl.when(c == pl.num_programs(1) - 1)
        def _():
            out_ref[...] = acc_ref[...]

    @pl.when((i >= nu_ref[0]) & (c == pl.num_programs(1) - 1))
    def _():
        out_ref[...] = jnp.zeros_like(out_ref)


def _expert_swiglu(xs, tile_expert, n_used, wg_b, wu_b, wd_b, tm, fc):
    n_rows, d = xs.shape
    d_ff = wg_b.shape[2]
    n_tiles = n_rows // tm
    n_chunks = d_ff // fc

    def live(i, nu):
        return jnp.minimum(i, nu[0] - 1)

    def chunk(i, c, nu):
        return jnp.where(i < nu[0], c, n_chunks - 1)

    grid_spec = pltpu.PrefetchScalarGridSpec(
        num_scalar_prefetch=2,
        grid=(n_tiles, n_chunks),
        in_specs=[
            pl.BlockSpec((tm, d), lambda i, c, te, nu: (live(i, nu), 0)),
            pl.BlockSpec((1, d, fc), lambda i, c, te, nu: (te[live(i, nu)], 0, chunk(i, c, nu))),
            pl.BlockSpec((1, d, fc), lambda i, c, te, nu: (te[live(i, nu)], 0, chunk(i, c, nu))),
            pl.BlockSpec((1, fc, d), lambda i, c, te, nu: (te[live(i, nu)], chunk(i, c, nu), 0)),
        ],
        out_specs=pl.BlockSpec((tm, d), lambda i, c, te, nu: (i, 0)),
        scratch_shapes=[pltpu.VMEM((tm, d), BF16), pltpu.VMEM((tm, d), F32)],
    )
    return pl.pallas_call(
        _expert_kernel,
        grid_spec=grid_spec,
        out_shape=jax.ShapeDtypeStruct((n_rows, d), F32),
        compiler_params=_cparams(("arbitrary", "arbitrary")),
    )(tile_expert, n_used, xs, wg_b, wu_b, wd_b)


def _combine_kernel(pos_ref, x_ref, info_ref, gt_ref, gf_ref, y_ref, out_ref, ybuf, sem):
    tc = x_ref.shape[0]

    def issue(r, c):
        for k in range(TOP_K):
            pltpu.make_async_copy(y_ref.at[pl.ds(pos_ref[0, 0, TOP_K * r + k], 1)],
                                  ybuf.at[k, pl.ds(r, 1)], sem).start()
        return c

    lax.fori_loop(0, tc, issue, 0, unroll=min(8, tc))
    for k in range(TOP_K):
        pltpu.make_async_copy(y_ref.at[pl.ds(0, tc)], ybuf.at[k], sem).wait()
    info = info_ref[...]
    moe = info[:, 0:1] * ybuf[0] + info[:, 1:2] * ybuf[1]
    x = x_ref[...] + gt_ref[0] * moe
    out_ref[...] = _rms(x, gf_ref[...])


def _combine(x, info, gate, final_g, y_sorted, pos, tc, rows_per_group):
    n, d = x.shape
    n_tiles = n // tc
    return pl.pallas_call(
        _combine_kernel,
        grid=(n_tiles,),
        in_specs=[
            pl.BlockSpec((1, 1, TOP_K * tc), lambda i: (i, 0, 0), memory_space=pltpu.SMEM),
            pl.BlockSpec((tc, d), lambda i: (i, 0)),
            pl.BlockSpec((tc, LANES), lambda i: (i, 0)),
            _mod_spec(gate, tc, rows_per_group),
            _resident((1, d), lambda i: (0, 0)),
            pl.BlockSpec(memory_space=pl.ANY),
        ],
        out_specs=pl.BlockSpec((tc, d), lambda i: (i, 0)),
        out_shape=jax.ShapeDtypeStruct((n, d), F32),
        scratch_shapes=[pltpu.VMEM((TOP_K, tc, d), F32), pltpu.SemaphoreType.DMA(())],
        compiler_params=_cparams(("arbitrary",)),
    )(pos.reshape(n_tiles, 1, TOP_K * tc), x, info, gate, final_g, y_sorted)


def _expert_layout(experts, tm):
    n_assign = experts.shape[0]
    n_tiles = n_assign // tm + N_EXPERTS
    onehot = (experts[:, None] == jnp.arange(N_EXPERTS, dtype=jnp.int32)[None, :]).astype(jnp.int32)
    running = jnp.cumsum(onehot, axis=0)
    counts = running[-1]
    rank = jnp.sum(running * onehot, axis=1) - 1
    tiles_per_expert = (counts + tm - 1) // tm
    tile_end = jnp.cumsum(tiles_per_expert)
    row_start = (tile_end - tiles_per_expert) * tm
    pos = row_start[experts] + rank
    n_used = tile_end[-1:]
    tile_ids = jnp.arange(n_tiles, dtype=jnp.int32)
    tile_expert = jnp.minimum(
        jnp.sum((tile_ids[:, None] >= tile_end[None, :]).astype(jnp.int32), axis=1),
        N_EXPERTS - 1).astype(jnp.int32)
    return pos.astype(jnp.int32), tile_expert, n_used.astype(jnp.int32), n_tiles


def kernel(x_prompt, x_sample, cache_k, cache_v, page_table, c_prompt, c_sample, norm_g, ada_w, ada_b, final_g, attn_w_qkv, attn_w_o, attn_b_score, sgu_w_in, sgu_g_v, sgu_w_s, sgu_b_s, sgu_w_out, ffn_w_gate, ffn_w_up, ffn_w_down, moe_w_router, moe_w_gate, moe_w_up, moe_w_down):
    b, t, d = x_prompt.shape
    s, t_new, _ = x_sample.shape
    assert t_new == 1, "the sample group decodes one token per sequence"
    n_p = b * t

    n_c = b + s
    n_c_pad = -(-n_c // 8) * 8
    c_all = jnp.concatenate(
        [c_prompt, c_sample, jnp.zeros((n_c_pad - n_c, d), F32)], axis=0)
    mods = _ada_modulation(c_all, ada_w, ada_b)

    def mod_parts(layer, sub):
        m = mods[2 * layer + sub]
        parts = []
        for j in range(3):
            col = m[:, j * d:(j + 1) * d]
            parts.append((col[0:b].reshape(b, 1, d), col[b:b + s].reshape(1, s, d)))
        return parts

    bf = lambda w: w.astype(BF16)
    g_row = lambda layer, sub: norm_g[layer, sub].reshape(1, d)
    xp = x_prompt.reshape(n_p, d)
    xs = x_sample.reshape(s, d)

    (sh_p, sh_s), (sc_p, sc_s), (gt0_p, gt0_s) = mod_parts(0, 0)
    w_qkv_b = bf(attn_w_qkv)
    q_p, kb_p, vb_p, k_p, v_p = _qkv(xp, g_row(0, 0), sc_p, sh_p, w_qkv_b, TM_QKV, t, True)
    q_s, kb_s, vb_s, k_s, v_s = _qkv(xs, g_row(0, 0), sc_s, sh_s, w_qkv_b, s, s, False)

    def neg_upper(n):
        i = jnp.arange(n, dtype=jnp.int32)
        return -(i[:, None] >= i[None, :]).astype(BF16)

    bias2 = attn_b_score * LOG2E
    o_p = _sb_attention(q_p.reshape(b, t, d), kb_p.reshape(b, t, d),
                        vb_p.reshape(b, t, d), bias2, neg_upper(TK_ATTN), TQ_ATTN)
    keys_last = lambda c: c.transpose(0, 2, 3, 1).reshape(c.shape[0], d, PAGE_SIZE)

    def new_page(a):
        return jnp.pad(a[:, :, None], ((0, 0), (0, 0), (0, PAGE_SIZE - 1)))

    o_s = _decode_attention(
        q_s.astype(F32).reshape(s, 1, d), new_page(k_s), new_page(v_s),
        keys_last(cache_k), keys_last(cache_v), page_table, bias2,
        neg_upper(PAGE_SIZE)).astype(BF16)

    (sh_p, sh_s), (sc_p, sc_s), (gt1_p, gt1_s) = mod_parts(0, 1)
    w_o_b, wg_b, wu_b, wd_b = bf(attn_w_o), bf(ffn_w_gate), bf(ffn_w_up), bf(ffn_w_down)
    xp = _attn_out_ffn(o_p.reshape(n_p, d), xp, gt0_p, w_o_b, g_row(0, 1), sc_p, sh_p,
                       gt1_p, wg_b, wu_b, wd_b, TM_FFN, t)
    xs = _attn_out_ffn(o_s.reshape(s, d), xs, gt0_s, w_o_b, g_row(0, 1), sc_s, sh_s,
                       gt1_s, wg_b, wu_b, wd_b, s, s)

    (sh_p, sh_s), (sc_p, sc_s), (gt_p, gt_s) = mod_parts(1, 0)
    width = sgu_g_v.shape[0]
    n_groups = sgu_w_s.shape[0]
    win_b, wout_b = bf(sgu_w_in), bf(sgu_w_out)
    g_v = sgu_g_v.reshape(1, width)
    bias_rows = jnp.broadcast_to(sgu_b_s.T[:, :, None],
                                 (CHUNK, n_groups, SGU_GROUP_DIM)).reshape(CHUNK, width)
    xp = _sgu_prompt(xp, g_row(1, 0), sc_p, sh_p, gt_p, win_b, g_v, sgu_w_s,
                     bias_rows, wout_b, TM_SGU, t)
    ws0_row = jnp.broadcast_to(sgu_w_s[:, 0, 0][:, None],
                               (n_groups, SGU_GROUP_DIM)).reshape(1, width)
    xs, sgu_v_s = _sgu_decode(xs, g_row(1, 0), sc_s, sh_s, gt_s, win_b, g_v,
                              ws0_row, bias_rows[0:1], wout_b)

    (sh_p, sh_s), (sc_p, sc_s), (gt_p, gt_s) = mod_parts(1, 1)
    wr_pad = jnp.concatenate(
        [moe_w_router, jnp.zeros((d, LANES - N_EXPERTS), F32)], axis=1)
    h_p, info_p = _route(xp, g_row(1, 1), sc_p, sh_p, wr_pad, TM_ROUTE, t)
    h_s, info_s = _route(xs, g_row(1, 1), sc_s, sh_s, wr_pad, s, s)

    experts = jnp.concatenate([info_p[:, 2:2 + TOP_K], info_s[:, 2:2 + TOP_K]],
                              axis=0).astype(jnp.int32).reshape(-1)
    pos, tile_expert, n_used, n_tiles = _expert_layout(experts, TM_MOE)
    x_sorted = _scatter_rows(h_p, pos[:TOP_K * n_p], TM_MOE, n_out=n_tiles * TM_MOE)
    x_sorted = _scatter_rows(h_s, pos[TOP_K * n_p:], s, dst=x_sorted)
    y_sorted = _expert_swiglu(x_sorted, tile_expert, n_used, bf(moe_w_gate),
                              bf(moe_w_up), bf(moe_w_down), TM_MOE, FC_MOE)

    fg = final_g.reshape(1, d)
    y_p = _combine(xp, info_p, gt_p, fg, y_sorted, pos[:TOP_K * n_p], TC_COMBINE, t)
    y_s = _combine(xs, info_s, gt_s, fg, y_sorted, pos[TOP_K * n_p:], s, s)

    heads = (N_HEADS, HEAD_DIM)
    return (y_p.reshape(b, t, d), y_s.reshape(s, 1, d),
            k_p.reshape(b, *heads, t).transpose(0, 3, 1, 2),
            v_p.reshape(b, *heads, t).transpose(0, 3, 1, 2),
            k_s.reshape(s, 1, *heads), v_s.reshape(s, 1, *heads),
            sgu_v_s.reshape(s, 1, width))
```

```python
import functools

import jax
import jax.numpy as jnp
from jax import lax
from jax.experimental import pallas as pl
from jax.experimental.pallas import tpu as pltpu

F32 = jnp.float32
BF16 = jnp.bfloat16

N_HEADS = 16
HEAD_DIM = 64
CHUNK = 128
SGU_GROUP_DIM = 128
N_EXPERTS = 8
TOP_K = 2
RMS_EPS = 1e-6
PAGE_SIZE = 128

LANES = 128
VMEM_LIMIT = 56 * 1024 * 1024

TM_QKV = 512
TM_FFN = 512
TM_SGU = 256
TM_ROUTE = 512
TQ_ATTN = 512
TK_ATTN = 256
TM_MOE = 512
FC_MOE = 1792
TC_COMBINE = 256
PAGES_PER_STEP = 8


def _cparams(sem):
    return pltpu.CompilerParams(dimension_semantics=sem, vmem_limit_bytes=VMEM_LIMIT)


def _resident(shape, index_map):
    return pl.BlockSpec(shape, index_map, pipeline_mode=pl.Buffered(1))


def _dot(a, b):
    return jnp.dot(a, b, preferred_element_type=F32)


def _rms(x, g):
    return x * lax.rsqrt(jnp.mean(x * x, axis=-1, keepdims=True) + RMS_EPS) * g


def _modnorm(x, g, scale, shift):
    return _rms(x, g) * (1.0 + scale) + shift


def _sigmoid(x):
    return 1.0 / (1.0 + jnp.exp(-x))


def _silu(x):
    return x * _sigmoid(x)


def _gelu_tanh(x):
    c = 0.7978845608028654
    return 0.5 * x * (1.0 + jnp.tanh(c * (x + 0.044715 * (x * x * x))))


LOG2E = 1.4426950408889634
MASKED_LOG = -1e30


def _neg_abs(z):
    bits = lax.bitcast_convert_type(z, jnp.uint32) | jnp.uint32(0x80000000)
    return lax.bitcast_convert_type(bits, F32)


def _mod_spec(mod, tm, rows_per_group):
    _, r, d = mod.shape
    tiles_per_group = rows_per_group // tm
    return pl.BlockSpec((1, r, d), lambda i: (i // tiles_per_group, 0, 0))


def _ada_kernel(c_ref, w_ref, b_ref, o_ref):
    s = _silu(c_ref[...])
    o_ref[0] = jnp.dot(s, w_ref[0], preferred_element_type=F32,
                       precision=lax.Precision.HIGHEST) + b_ref[0]


def _ada_modulation(c_all, ada_w, ada_b):
    r, d = c_all.shape
    n_mod = ada_w.shape[0] * ada_w.shape[1]
    w = ada_w.reshape(n_mod, d, 3 * d)
    b = ada_b.reshape(n_mod, 1, 3 * d)
    tn = d
    return pl.pallas_call(
        _ada_kernel,
        grid=(n_mod, 3 * d // tn),
        in_specs=[
            pl.BlockSpec((r, d), lambda i, j: (0, 0)),
            pl.BlockSpec((1, d, tn), lambda i, j: (i, 0, j)),
            pl.BlockSpec((1, 1, tn), lambda i, j: (i, 0, j)),
        ],
        out_specs=pl.BlockSpec((1, r, tn), lambda i, j: (i, 0, j)),
        out_shape=jax.ShapeDtypeStruct((n_mod, r, 3 * d), F32),
        compiler_params=_cparams(("arbitrary", "arbitrary")),
    )(c_all, w, b)


def _qkv_kernel(x_ref, g_ref, sc_ref, sh_ref, w_ref, q_ref, kb_ref, vb_ref, kf_ref, vf_ref,
                *, channels_first):
    d = x_ref.shape[1]
    h = _modnorm(x_ref[...], g_ref[...], sc_ref[0], sh_ref[0]).astype(BF16)
    q = _dot(h, w_ref[:, 0:d])
    q_ref[...] = (q * (HEAD_DIM ** -0.5 * LOG2E)).astype(BF16)
    k = _dot(h, w_ref[:, d:2 * d])
    kb_ref[...] = k.astype(BF16)
    v = _dot(h, w_ref[:, 2 * d:3 * d])
    vb_ref[...] = v.astype(BF16)
    if channels_first:
        kf_ref[0] = k.T
        vf_ref[0] = v.T
    else:
        kf_ref[...] = k
        vf_ref[...] = v


def _qkv(x, g, scale, shift, w_qkv_b, tm, rows_per_group, channels_first):
    n, d = x.shape
    row = pl.BlockSpec((tm, d), lambda i: (i, 0))
    if channels_first:
        tiles = rows_per_group // tm
        kv_spec = pl.BlockSpec((1, d, tm), lambda i: (i // tiles, 0, i % tiles))
        kv_shape = jax.ShapeDtypeStruct((n // rows_per_group, d, rows_per_group), F32)
    else:
        kv_spec, kv_shape = row, jax.ShapeDtypeStruct((n, d), F32)
    return pl.pallas_call(
        functools.partial(_qkv_kernel, channels_first=channels_first),
        grid=(n // tm,),
        in_specs=[
            row,
            _resident((1, d), lambda i: (0, 0)),
            _mod_spec(scale, tm, rows_per_group),
            _mod_spec(shift, tm, rows_per_group),
            _resident((d, 3 * d), lambda i: (0, 0)),
        ],
        out_specs=[row, row, row, kv_spec, kv_spec],
        out_shape=[
            jax.ShapeDtypeStruct((n, d), BF16),
            jax.ShapeDtypeStruct((n, d), BF16),
            jax.ShapeDtypeStruct((n, d), BF16),
            kv_shape,
            kv_shape,
        ],
        compiler_params=_cparams(("arbitrary",)),
    )(x, g, scale, shift, w_qkv_b)


def _softplus2(z):
    return jnp.maximum(z, 0.0) + jnp.log2(1.0 + jnp.exp2(_neg_abs(z)))


def _prompt_attn_body(bias_ref, q_ref, k_ref, v_ref, u_ref, o_ref,
                      acc_ref, carry_ref, zs_ref, sp_ref, a_ref, vh_ref):
    tq = q_ref.shape[1]
    tk = u_ref.shape[0]
    assert tq == 2 * tk, "the pipeline below is written for two key tiles per query block"
    hp = pl.program_id(1)
    qi = pl.program_id(2)

    lane = lax.broadcasted_iota(jnp.int32, (1, LANES), 1)
    first = (lane < HEAD_DIM).astype(F32)
    head_masks = (first.astype(BF16), (1.0 - first).astype(BF16))
    q = q_ref[0]
    qhs = [q * m for m in head_masks]
    biases = [bias_ref[2 * hp], bias_ref[2 * hp + 1]]
    neg_from_here = u_ref[...]

    acc_ref[...] = jnp.zeros_like(acc_ref)
    carry_ref[...] = jnp.zeros_like(carry_ref)

    row = lax.broadcasted_iota(jnp.int32, (tq, tk), 0)
    col = lax.broadcasted_iota(jnp.int32, (tq, tk), 1)
    newest_tile = 2 * qi + 1

    def key_start(item):
        return pl.multiple_of((newest_tile - item) * tk, tk)

    def score(item, slot, diag_offset=None):
        kblk = k_ref[0, pl.ds(key_start(item), tk), :]
        for hh in range(2):
            z = lax.dot_general(qhs[hh], kblk, (((1,), (1,)), ((), ())),
                                preferred_element_type=F32) + biases[hh]
            sp = _softplus2(z)
            if diag_offset is not None:
                valid = (col + diag_offset) < row
                sp = jnp.where(valid, sp, 0.0)
                z = jnp.where(valid, z, MASKED_LOG)
            zs_ref[slot, hh] = z
            sp_ref[slot, hh] = sp.astype(BF16)

    def weights(slot):
        for hh in range(2):
            from_here = _dot(sp_ref[slot, hh], neg_from_here)
            c = carry_ref[hh]
            a_ref[slot, hh] = jnp.exp2(zs_ref[slot, hh] + from_here + c).astype(BF16)
            carry_ref[hh] = c + from_here[:, 0:1]

    @pl.when(qi == 0)
    def _():
        for hh in range(2):
            vh_ref[hh] = v_ref[0] * head_masks[hh]

    def values(item, slot):
        keys = pl.ds(key_start(item), tk)
        acc_ref[...] += (_dot(a_ref[slot, 0], vh_ref[0, keys, :])
                         + _dot(a_ref[slot, 1], vh_ref[1, keys, :]))

    score(0, 0, diag_offset=tk)
    score(1, 1, diag_offset=0)
    weights(0)

    def body(jj, c):
        j = 2 + 2 * jj
        score(j, 0)
        weights(1)
        values(j - 2, 0)
        score(j + 1, 1)
        weights(0)
        values(j - 1, 1)
        return c

    lax.fori_loop(0, qi, body, 0)
    n_tiles = 2 + 2 * qi
    weights(1)
    values(n_tiles - 2, 0)
    values(n_tiles - 1, 1)
    o_ref[0] = acc_ref[...].astype(o_ref.dtype)


def _decode_attn_body(p, last_p, bias_ref, q_ref, kn_ref, vn_ref, u_ref, k_refs, v_refs,
                      o_ref, acc_ref, carry_ref, qblk_ref):
    d = q_ref.shape[2]

    neg_from_here = u_ref[...]
    bias = bias_ref[...]
    head_of_lane = lax.broadcasted_iota(jnp.int32, (N_HEADS, d), 1) // HEAD_DIM
    own = head_of_lane == lax.broadcasted_iota(jnp.int32, (N_HEADS, d), 0)
    nt = (((1,), (1,)), ((), ()))

    def visit(pages, valid):
        scored = []
        for k_ref, _ in pages:
            z = _dot(qblk_ref[...], k_ref[0].astype(BF16)) + bias
            sp = _softplus2(z)
            if valid is not None:
                sp = jnp.where(valid, sp, 0.0)
                z = jnp.where(valid, z, MASKED_LOG)
            scored.append((z, sp.astype(BF16)))
        sums = [_dot(spb, neg_from_here) for _, spb in scored]
        c = carry_ref[...]
        weights = []
        for (z, _), from_here in zip(scored, sums):
            weights.append(jnp.exp2(z + from_here + c).astype(BF16))
            c = c + from_here[:, 0:1]
        carry_ref[...] = c
        o = acc_ref[...]
        for (_, v_ref), a in zip(pages, weights):
            o = o + lax.dot_general(a, v_ref[0].astype(BF16), nt, preferred_element_type=F32)
        acc_ref[...] = o

    @pl.when(p == 0)
    def _():
        acc_ref[...] = jnp.zeros_like(acc_ref)
        carry_ref[...] = jnp.zeros_like(carry_ref)
        qb = jnp.broadcast_to(q_ref[0], (N_HEADS, d))
        qblk_ref[...] = jnp.where(own, qb, 0.0).astype(BF16)
        n_new = 1
        q_index = 0
        key_pos = lax.broadcasted_iota(jnp.int32, (N_HEADS, PAGE_SIZE), 1)
        visit([(kn_ref, vn_ref)], (key_pos < n_new) & (key_pos < q_index))

    visit(list(zip(k_refs, v_refs)), None)

    @pl.when(p == last_p)
    def _():
        o_ref[0] = jnp.sum(jnp.where(own, acc_ref[...], 0.0), axis=0, keepdims=True)


def _attention_kernel(pt_ref, bias_ref, q_ref, k_ref, v_ref, u_ref,
                      dbias_ref, dq_ref, kn_ref, vn_ref, du_ref, *rest, steps_per_seq):
    npg = PAGES_PER_STEP
    k_pages = rest[0:npg]
    v_pages = rest[npg:2 * npg]
    o_ref, do_ref = rest[2 * npg:2 * npg + 2]
    (acc_ref, carry_ref, zs_ref, sp_ref, a_ref, vh_ref,
     dacc_ref, dcarry_ref, qblk_ref) = rest[2 * npg + 2:]
    step = ((pl.program_id(0) * pl.num_programs(1) + pl.program_id(1)) * pl.num_programs(2)
            + pl.program_id(2))
    _decode_attn_body(step % steps_per_seq, steps_per_seq - 1, dbias_ref, dq_ref, kn_ref,
                      vn_ref, du_ref, k_pages, v_pages, do_ref, dacc_ref, dcarry_ref, qblk_ref)
    _prompt_attn_body(bias_ref, q_ref, k_ref, v_ref, u_ref, o_ref,
                      acc_ref, carry_ref, zs_ref, sp_ref, a_ref, vh_ref)


def _attention(q, k, v, bias2, u_tile, tq, dq, k_new, v_new, cache_k, cache_v, page_table,
               u_page):
    b, t, d = q.shape
    tk = u_tile.shape[0]
    n_pairs = d // LANES
    nq = t // tq
    s = dq.shape[0]
    n_pages = page_table.shape[1]
    npg = PAGES_PER_STEP
    steps_per_seq = n_pages // npg
    assert n_pages % npg == 0 and s * steps_per_seq == b * n_pairs * nq, (
        "the decode sweep is spread over exactly the prompt sweep's grid steps")
    pt_flat = page_table.reshape(-1)

    def linear(bi, hp, qi):
        return (bi * n_pairs + hp) * nq + qi

    def seq_map(bi, hp, qi, pt):
        return (linear(bi, hp, qi) // steps_per_seq, 0, 0)

    def page_spec(i):
        def index_map(bi, hp, qi, pt):
            step = linear(bi, hp, qi)
            si, p = step // steps_per_seq, step % steps_per_seq
            return (pt[si * n_pages + (n_pages - 1 - (p * npg + i))], 0, 0)
        return pl.BlockSpec((1, d, PAGE_SIZE), index_map)

    fixed = lambda bi, hp, qi, pt: (0, 0)
    per_seq = pl.BlockSpec((1, 1, d), seq_map)
    new_page = pl.BlockSpec((1, d, PAGE_SIZE), seq_map)
    q_block = pl.BlockSpec((1, tq, LANES), lambda bi, hp, qi, pt: (bi, qi, hp))
    kv_block = pl.BlockSpec((1, t, LANES), lambda bi, hp, qi, pt: (bi, 0, hp))
    grid_spec = pltpu.PrefetchScalarGridSpec(
        num_scalar_prefetch=1,
        grid=(b, n_pairs, nq),
        in_specs=[
            pl.BlockSpec(memory_space=pltpu.SMEM),
            q_block, kv_block, kv_block,
            _resident(u_tile.shape, fixed),
            _resident((N_HEADS, 1), fixed),
            per_seq, new_page, new_page,
            _resident(u_page.shape, fixed),
        ] + [page_spec(i) for i in range(npg)] + [page_spec(i) for i in range(npg)],
        out_specs=[q_block, per_seq],
        scratch_shapes=[
            pltpu.VMEM((tq, LANES), F32),
            pltpu.VMEM((2, tq, 1), F32),
            pltpu.VMEM((2, 2, tq, tk), F32),
            pltpu.VMEM((2, 2, tq, tk), BF16),
            pltpu.VMEM((2, 2, tq, tk), BF16),
            pltpu.VMEM((2, t, LANES), BF16),
            pltpu.VMEM((N_HEADS, d), F32),
            pltpu.VMEM((N_HEADS, 1), F32),
            pltpu.VMEM((N_HEADS, d), BF16),
        ],
    )
    return pl.pallas_call(
        functools.partial(_attention_kernel, steps_per_seq=steps_per_seq),
        grid_spec=grid_spec,
        out_shape=[jax.ShapeDtypeStruct((b, t, d), BF16),
                   jax.ShapeDtypeStruct((s, 1, d), F32)],
        compiler_params=_cparams(("arbitrary", "arbitrary", "arbitrary")),
    )(pt_flat, bias2, q, k, v, u_tile, bias2.reshape(N_HEADS, 1), dq, k_new, v_new, u_page,
      *([cache_k] * npg), *([cache_v] * npg))


def _ffn_kernel(o_ref, x_ref, gt0_ref, wo_ref, g_ref, sc_ref, sh_ref, gt1_ref,
                wg_ref, wu_ref, wd_ref, out_ref, *, ff_chunk):
    x1 = x_ref[...] + gt0_ref[0] * _dot(o_ref[...], wo_ref[...])
    h = _modnorm(x1, g_ref[...], sc_ref[0], sh_ref[0]).astype(BF16)
    d_ff = wg_ref.shape[1]
    f = None
    for c0 in range(0, d_ff, ff_chunk):
        hg = _dot(h, wg_ref[:, c0:c0 + ff_chunk])
        hu = _dot(h, wu_ref[:, c0:c0 + ff_chunk])
        part = _dot((_silu(hg) * hu).astype(BF16), wd_ref[c0:c0 + ff_chunk, :])
        f = part if f is None else f + part
    out_ref[...] = x1 + gt1_ref[0] * f


def _attn_out_ffn(o, x, gate0, w_o_b, g, scale, shift, gate1, wg_b, wu_b, wd_b,
                  tm, rows_per_group):
    n, d = x.shape
    d_ff = wg_b.shape[1]
    row = pl.BlockSpec((tm, d), lambda i: (i, 0))
    fixed = lambda i: (0, 0)
    mod = lambda m: _mod_spec(m, tm, rows_per_group)
    return pl.pallas_call(
        functools.partial(_ffn_kernel, ff_chunk=d_ff // 2),
        grid=(n // tm,),
        in_specs=[
            row, row, mod(gate0), _resident((d, d), fixed),
            _resident((1, d), fixed), mod(scale), mod(shift), mod(gate1),
            _resident((d, d_ff), fixed), _resident((d, d_ff), fixed),
            _resident((d_ff, d), fixed),
        ],
        out_specs=row,
        out_shape=jax.ShapeDtypeStruct((n, d), F32),
        compiler_params=_cparams(("arbitrary",)),
    )(o, x, gate0, w_o_b, g, scale, shift, gate1, wg_b, wu_b, wd_b)


def _sgu_uv(x_ref, g_ref, sc_ref, sh_ref, win_ref, gv_ref):
    width = gv_ref.shape[1]
    h = _modnorm(x_ref[...], g_ref[...], sc_ref[0], sh_ref[0]).astype(BF16)
    u = _gelu_tanh(_dot(h, win_ref[:, 0:width]))
    v = _rms(_gelu_tanh(_dot(h, win_ref[:, width:2 * width])), gv_ref[...])
    return u, v


def _sgu_prompt_kernel(x_ref, g_ref, sc_ref, sh_ref, gt_ref, win_ref, gv_ref,
                       ws_ref, bs_ref, wout_ref, out_ref, mixed_ref):
    tm = x_ref.shape[0]
    n_groups = ws_ref.shape[0]
    u, v = _sgu_uv(x_ref, g_ref, sc_ref, sh_ref, win_ref, gv_ref)
    vb = v.astype(BF16)
    i_pos = lax.broadcasted_iota(jnp.int32, (CHUNK, CHUNK), 0)
    j_pos = lax.broadcasted_iota(jnp.int32, (CHUNK, CHUNK), 1)
    causal = j_pos <= i_pos
    for gi in range(n_groups):
        ws = jnp.where(causal, ws_ref[gi], 0.0).astype(BF16)
        c0 = gi * SGU_GROUP_DIM
        for ci in range(tm // CHUNK):
            r0 = ci * CHUNK
            mixed_ref[r0:r0 + CHUNK, c0:c0 + SGU_GROUP_DIM] = _dot(
                ws, vb[r0:r0 + CHUNK, c0:c0 + SGU_GROUP_DIM])
    bias = bs_ref[...]
    for ci in range(tm // CHUNK):
        r0 = ci * CHUNK
        mixed_ref[r0:r0 + CHUNK, :] += bias
    y = _dot((u * mixed_ref[...]).astype(BF16), wout_ref[...])
    out_ref[...] = x_ref[...] + gt_ref[0] * y


def _sgu_prompt(x, g, scale, shift, gate, win_b, g_v, w_s, bias_rows, wout_b,
                tm, rows_per_group):
    n, d = x.shape
    width = g_v.shape[1]
    row = pl.BlockSpec((tm, d), lambda i: (i, 0))
    fixed = lambda i: (0, 0)
    mod = lambda m: _mod_spec(m, tm, rows_per_group)
    return pl.pallas_call(
        _sgu_prompt_kernel,
        grid=(n // tm,),
        in_specs=[
            row, _resident((1, d), fixed), mod(scale), mod(shift), mod(gate),
            _resident((d, 2 * width), fixed), _resident((1, width), fixed),
            _resident(w_s.shape, lambda i: (0, 0, 0)),
            _resident(bias_rows.shape, fixed), _resident((width, d), fixed),
        ],
        out_specs=row,
        out_shape=jax.ShapeDtypeStruct((n, d), F32),
        scratch_shapes=[pltpu.VMEM((tm, width), F32)],
        compiler_params=_cparams(("arbitrary",)),
    )(x, g, scale, shift, gate, win_b, g_v, w_s, bias_rows, wout_b)


def _sgu_decode_kernel(x_ref, g_ref, sc_ref, sh_ref, gt_ref, win_ref, gv_ref,
                       ws0_ref, bs0_ref, wout_ref, out_ref, v_ref):
    u, v = _sgu_uv(x_ref, g_ref, sc_ref, sh_ref, win_ref, gv_ref)
    v_ref[...] = v
    mixed = ws0_ref[...] * v + bs0_ref[...]
    y = _dot((u * mixed).astype(BF16), wout_ref[...])
    out_ref[...] = x_ref[...] + gt_ref[0] * y


def _sgu_decode(x, g, scale, shift, gate, win_b, g_v, ws0_row, bs0_row, wout_b):
    n, d = x.shape
    width = g_v.shape[1]
    whole = lambda a: pl.BlockSpec(a.shape, lambda i: (0,) * a.ndim)
    args = (x, g, scale, shift, gate, win_b, g_v, ws0_row, bs0_row, wout_b)
    return pl.pallas_call(
        _sgu_decode_kernel,
        grid=(1,),
        in_specs=[whole(a) for a in args],
        out_specs=[pl.BlockSpec((n, d), lambda i: (0, 0)),
                   pl.BlockSpec((n, width), lambda i: (0, 0))],
        out_shape=[jax.ShapeDtypeStruct((n, d), F32),
                   jax.ShapeDtypeStruct((n, width), F32)],
        compiler_params=_cparams(("arbitrary",)),
    )(*args)


def _route_kernel(x_ref, g_ref, sc_ref, sh_ref, wr_ref, h_ref, info_ref):
    h = _modnorm(x_ref[...], g_ref[...], sc_ref[0], sh_ref[0])
    h_ref[...] = h
    logits = jnp.dot(h, wr_ref[...], preferred_element_type=F32,
                     precision=lax.Precision.HIGHEST)
    lane = lax.broadcasted_iota(jnp.int32, logits.shape, 1)
    lane_f = lane.astype(F32)
    neg = jnp.float32(-jnp.inf)
    l1 = jnp.where(lane < N_EXPERTS, logits, neg)
    m1 = jnp.max(l1, axis=-1, keepdims=True)
    i1 = jnp.min(jnp.where(l1 == m1, lane_f, float(LANES)), axis=-1, keepdims=True)
    l2 = jnp.where(lane_f == i1, neg, l1)
    m2 = jnp.max(l2, axis=-1, keepdims=True)
    i2 = jnp.min(jnp.where(l2 == m2, lane_f, float(LANES)), axis=-1, keepdims=True)
    e = jnp.exp(m2 - m1)
    g1 = 1.0 / (1.0 + e)
    g2 = e / (1.0 + e)
    info = jnp.where(lane == 0, g1, 0.0)
    info = jnp.where(lane == 1, g2, info)
    info = jnp.where(lane == 2, i1, info)
    info = jnp.where(lane == 3, i2, info)
    info_ref[...] = info


def _route(x, g, scale, shift, w_router_pad, tm, rows_per_group):
    n, d = x.shape
    row = pl.BlockSpec((tm, d), lambda i: (i, 0))
    fixed = lambda i: (0, 0)
    return pl.pallas_call(
        _route_kernel,
        grid=(n // tm,),
        in_specs=[
            row, _resident((1, d), fixed),
            _mod_spec(scale, tm, rows_per_group), _mod_spec(shift, tm, rows_per_group),
            _resident((d, LANES), fixed),
        ],
        out_specs=[row, pl.BlockSpec((tm, LANES), lambda i: (i, 0))],
        out_shape=[jax.ShapeDtypeStruct((n, d), F32),
                   jax.ShapeDtypeStruct((n, LANES), F32)],
        compiler_params=_cparams(("arbitrary",)),
    )(x, g, scale, shift, w_router_pad)


def _scatter_rows_kernel(pos_ref, src_ref, *rest, tm, zero_fill):
    if zero_fill:
        out_ref, zero_ref, sem = rest
    else:
        _, out_ref, sem = rest
    i = pl.program_id(0)
    n_copies = TOP_K * tm

    if zero_fill:
        @pl.when(i == 0)
        def _():
            fill = zero_ref.shape[0]
            zero_ref[...] = jnp.zeros_like(zero_ref)

            def fill_copy(j):
                return pltpu.make_async_copy(
                    zero_ref, out_ref.at[pl.ds(pl.multiple_of(j * fill, fill), fill)], sem)

            def start(j, c):
                fill_copy(j).start()
                return c

            def wait(j, c):
                fill_copy(j).wait()
                return c

            lax.fori_loop(0, out_ref.shape[0] // fill, start, 0)
            lax.fori_loop(0, out_ref.shape[0] // fill, wait, 0)

    def issue(r, c):
        for k in range(TOP_K):
            pltpu.make_async_copy(src_ref.at[pl.ds(r, 1)],
                                  out_ref.at[pl.ds(pos_ref[0, 0, TOP_K * r + k], 1)],
                                  sem).start()
        return c

    lax.fori_loop(0, tm, issue, 0, unroll=8)
    pltpu.make_async_copy(out_ref.at[pl.ds(0, n_copies)],
                          out_ref.at[pl.ds(n_copies, n_copies)], sem).wait()


def _scatter_rows(src, pos, tm, n_out=None, dst=None):
    n, d = src.shape
    n_tiles = n // tm
    zero_fill = dst is None
    n_out = n_out if zero_fill else dst.shape[0]
    any_spec = pl.BlockSpec(memory_space=pl.ANY)
    pos_spec = pl.BlockSpec((1, 1, TOP_K * tm), lambda i: (i, 0, 0), memory_space=pltpu.SMEM)
    scratch = [pltpu.SemaphoreType.DMA(())]
    if zero_fill:
        scratch = [pltpu.VMEM((TM_MOE, d), src.dtype)] + scratch
    return pl.pallas_call(
        functools.partial(_scatter_rows_kernel, tm=tm, zero_fill=zero_fill),
        grid=(n_tiles,),
        in_specs=[pos_spec, pl.BlockSpec((tm, d), lambda i: (i, 0))]
        + ([] if zero_fill else [any_spec]),
        out_specs=any_spec,
        out_shape=jax.ShapeDtypeStruct((n_out, d), src.dtype),
        scratch_shapes=scratch,
        input_output_aliases={} if zero_fill else {2: 0},
        compiler_params=_cparams(("arbitrary",)),
    )(pos.reshape(n_tiles, 1, TOP_K * tm), src, *([] if zero_fill else [dst]))


def _expert_kernel(te_ref, nu_ref, xs_ref, wg_ref, wu_ref, wd_ref, out_ref,
                   xb_ref, acc_ref):
    i = pl.program_id(0)
    c = pl.program_id(1)

    @pl.when(i < nu_ref[0])
    def _():
        @pl.when(c == 0)
        def _():
            xb_ref[...] = xs_ref[...].astype(BF16)
            acc_ref[...] = jnp.zeros_like(acc_ref)

        xb = xb_ref[...]
        hg = _dot(xb, wg_ref[0])
        hu = _dot(xb, wu_ref[0])
        acc_ref[...] += _dot((_silu(hg) * hu).astype(BF16), wd_ref[0])

        @pl.when(c == pl.num_programs(1) - 1)
        def _():
            out_ref[...] = acc_ref[...]

    @pl.when((i >= nu_ref[0]) & (c == pl.num_programs(1) - 1))
    def _():
        out_ref[...] = jnp.zeros_like(out_ref)


def _expert_swiglu(xs, tile_expert, n_used, wg_b, wu_b, wd_b, tm, fc):
    n_rows, d = xs.shape
    d_ff = wg_b.shape[2]
    n_tiles = n_rows // tm
    n_chunks = d_ff // fc

    def live(i, nu):
        return jnp.minimum(i, nu[0] - 1)

    def chunk(i, c, nu):
        return jnp.where(i < nu[0], c, n_chunks - 1)

    grid_spec = pltpu.PrefetchScalarGridSpec(
        num_scalar_prefetch=2,
        grid=(n_tiles, n_chunks),
        in_specs=[
            pl.BlockSpec((tm, d), lambda i, c, te, nu: (live(i, nu), 0)),
            pl.BlockSpec((1, d, fc), lambda i, c, te, nu: (te[live(i, nu)], 0, chunk(i, c, nu))),
            pl.BlockSpec((1, d, fc), lambda i, c, te, nu: (te[live(i, nu)], 0, chunk(i, c, nu))),
            pl.BlockSpec((1, fc, d), lambda i, c, te, nu: (te[live(i, nu)], chunk(i, c, nu), 0)),
        ],
        out_specs=pl.BlockSpec((tm, d), lambda i, c, te, nu: (i, 0)),
        scratch_shapes=[pltpu.VMEM((tm, d), BF16), pltpu.VMEM((tm, d), F32)],
    )
    return pl.pallas_call(
        _expert_kernel,
        grid_spec=grid_spec,
        out_shape=jax.ShapeDtypeStruct((n_rows, d), F32),
        compiler_params=_cparams(("arbitrary", "arbitrary")),
    )(tile_expert, n_used, xs, wg_b, wu_b, wd_b)


def _combine_kernel(pos_ref, x_ref, info_ref, gt_ref, gf_ref, y_ref, out_ref, ybuf, sem):
    tc = x_ref.shape[0]

    def issue(r, c):
        for k in range(TOP_K):
            pltpu.make_async_copy(y_ref.at[pl.ds(pos_ref[0, 0, TOP_K * r + k], 1)],
                                  ybuf.at[k, pl.ds(r, 1)], sem).start()
        return c

    lax.fori_loop(0, tc, issue, 0, unroll=min(8, tc))
    for k in range(TOP_K):
        pltpu.make_async_copy(y_ref.at[pl.ds(0, tc)], ybuf.at[k], sem).wait()
    info = info_ref[...]
    moe = info[:, 0:1] * ybuf[0] + info[:, 1:2] * ybuf[1]
    x = x_ref[...] + gt_ref[0] * moe
    out_ref[...] = _rms(x, gf_ref[...])


def _combine(x, info, gate, final_g, y_sorted, pos, tc, rows_per_group):
    n, d = x.shape
    n_tiles = n // tc
    return pl.pallas_call(
        _combine_kernel,
        grid=(n_tiles,),
        in_specs=[
            pl.BlockSpec((1, 1, TOP_K * tc), lambda i: (i, 0, 0), memory_space=pltpu.SMEM),
            pl.BlockSpec((tc, d), lambda i: (i, 0)),
            pl.BlockSpec((tc, LANES), lambda i: (i, 0)),
            _mod_spec(gate, tc, rows_per_group),
            _resident((1, d), lambda i: (0, 0)),
            pl.BlockSpec(memory_space=pl.ANY),
        ],
        out_specs=pl.BlockSpec((tc, d), lambda i: (i, 0)),
        out_shape=jax.ShapeDtypeStruct((n, d), F32),
        scratch_shapes=[pltpu.VMEM((TOP_K, tc, d), F32), pltpu.SemaphoreType.DMA(())],
        compiler_params=_cparams(("arbitrary",)),
    )(pos.reshape(n_tiles, 1, TOP_K * tc), x, info, gate, final_g, y_sorted)


def _expert_layout(experts, tm):
    n_assign = experts.shape[0]
    n_tiles = n_assign // tm + N_EXPERTS
    onehot = (experts[:, None] == jnp.arange(N_EXPERTS, dtype=jnp.int32)[None, :]).astype(jnp.int32)
    running = jnp.cumsum(onehot, axis=0)
    counts = running[-1]
    rank = jnp.sum(running * onehot, axis=1) - 1
    tiles_per_expert = (counts + tm - 1) // tm
    tile_end = jnp.cumsum(tiles_per_expert)
    row_start = (tile_end - tiles_per_expert) * tm
    pos = row_start[experts] + rank
    n_used = tile_end[-1:]
    tile_ids = jnp.arange(n_tiles, dtype=jnp.int32)
    tile_expert = jnp.minimum(
        jnp.sum((tile_ids[:, None] >= tile_end[None, :]).astype(jnp.int32), axis=1),
        N_EXPERTS - 1).astype(jnp.int32)
    return pos.astype(jnp.int32), tile_expert, n_used.astype(jnp.int32), n_tiles


def kernel(x_prompt, x_sample, cache_k, cache_v, page_table, c_prompt, c_sample, norm_g, ada_w, ada_b, final_g, attn_w_qkv, attn_w_o, attn_b_score, sgu_w_in, sgu_g_v, sgu_w_s, sgu_b_s, sgu_w_out, ffn_w_gate, ffn_w_up, ffn_w_down, moe_w_router, moe_w_gate, moe_w_up, moe_w_down):
    b, t, d = x_prompt.shape
    s, t_new, _ = x_sample.shape
    assert t_new == 1, "the sample group decodes one token per sequence"
    n_p = b * t

    n_c = b + s
    n_c_pad = -(-n_c // 8) * 8
    c_all = jnp.concatenate(
        [c_prompt, c_sample, jnp.zeros((n_c_pad - n_c, d), F32)], axis=0)
    mods = _ada_modulation(c_all, ada_w, ada_b)

    def mod_parts(layer, sub):
        m = mods[2 * layer + sub]
        parts = []
        for j in range(3):
            col = m[:, j * d:(j + 1) * d]
            parts.append((col[0:b].reshape(b, 1, d), col[b:b + s].reshape(1, s, d)))
        return parts

    bf = lambda w: w.astype(BF16)
    g_row = lambda layer, sub: norm_g[layer, sub].reshape(1, d)
    xp = x_prompt.reshape(n_p, d)
    xs = x_sample.reshape(s, d)

    (sh_p, sh_s), (sc_p, sc_s), (gt0_p, gt0_s) = mod_parts(0, 0)
    w_qkv_b = bf(attn_w_qkv)
    q_p, kb_p, vb_p, k_p, v_p = _qkv(xp, g_row(0, 0), sc_p, sh_p, w_qkv_b, TM_QKV, t, True)
    q_s, kb_s, vb_s, k_s, v_s = _qkv(xs, g_row(0, 0), sc_s, sh_s, w_qkv_b, s, s, False)

    def neg_upper(n):
        i = jnp.arange(n, dtype=jnp.int32)
        return -(i[:, None] >= i[None, :]).astype(BF16)

    bias2 = attn_b_score * LOG2E
    keys_last = lambda c: c.transpose(0, 2, 3, 1).reshape(c.shape[0], d, PAGE_SIZE)

    def new_page(a):
        return jnp.pad(a[:, :, None], ((0, 0), (0, 0), (0, PAGE_SIZE - 1)))

    o_p, o_s = _attention(
        q_p.reshape(b, t, d), kb_p.reshape(b, t, d), vb_p.reshape(b, t, d), bias2,
        neg_upper(TK_ATTN), TQ_ATTN,
        q_s.astype(F32).reshape(s, 1, d), new_page(k_s), new_page(v_s),
        keys_last(cache_k), keys_last(cache_v), page_table, neg_upper(PAGE_SIZE))
    o_s = o_s.astype(BF16)

    (sh_p, sh_s), (sc_p, sc_s), (gt1_p, gt1_s) = mod_parts(0, 1)
    w_o_b, wg_b, wu_b, wd_b = bf(attn_w_o), bf(ffn_w_gate), bf(ffn_w_up), bf(ffn_w_down)
    xp = _attn_out_ffn(o_p.reshape(n_p, d), xp, gt0_p, w_o_b, g_row(0, 1), sc_p, sh_p,
                       gt1_p, wg_b, wu_b, wd_b, TM_FFN, t)
    xs = _attn_out_ffn(o_s.reshape(s, d), xs, gt0_s, w_o_b, g_row(0, 1), sc_s, sh_s,
                       gt1_s, wg_b, wu_b, wd_b, s, s)

    (sh_p, sh_s), (sc_p, sc_s), (gt_p, gt_s) = mod_parts(1, 0)
    width = sgu_g_v.shape[0]
    n_groups = sgu_w_s.shape[0]
    win_b, wout_b = bf(sgu_w_in), bf(sgu_w_out)
    g_v = sgu_g_v.reshape(1, width)
    bias_rows = jnp.broadcast_to(sgu_b_s.T[:, :, None],
                                 (CHUNK, n_groups, SGU_GROUP_DIM)).reshape(CHUNK, width)
    xp = _sgu_prompt(xp, g_row(1, 0), sc_p, sh_p, gt_p, win_b, g_v, sgu_w_s,
                     bias_rows, wout_b, TM_SGU, t)
    ws0_row = jnp.broadcast_to(sgu_w_s[:, 0, 0][:, None],
                               (n_groups, SGU_GROUP_DIM)).reshape(1, width)
    xs, sgu_v_s = _sgu_decode(xs, g_row(1, 0), sc_s, sh_s, gt_s, win_b, g_v,
                              ws0_row, bias_rows[0:1], wout_b)

    (sh_p, sh_s), (sc_p, sc_s), (gt_p, gt_s) = mod_parts(1, 1)
    wr_pad = jnp.concatenate(
        [moe_w_router, jnp.zeros((d, LANES - N_EXPERTS), F32)], axis=1)
    h_p, info_p = _route(xp, g_row(1, 1), sc_p, sh_p, wr_pad, TM_ROUTE, t)
    h_s, info_s = _route(xs, g_row(1, 1), sc_s, sh_s, wr_pad, s, s)

    experts = jnp.concatenate([info_p[:, 2:2 + TOP_K], info_s[:, 2:2 + TOP_K]],
                              axis=0).astype(jnp.int32).reshape(-1)
    pos, tile_expert, n_used, n_tiles = _expert_layout(experts, TM_MOE)
    x_sorted = _scatter_rows(h_p, pos[:TOP_K * n_p], TM_MOE, n_out=n_tiles * TM_MOE)
    x_sorted = _scatter_rows(h_s, pos[TOP_K * n_p:], s, dst=x_sorted)
    y_sorted = _expert_swiglu(x_sorted, tile_expert, n_used, bf(moe_w_gate),
                              bf(moe_w_up), bf(moe_w_down), TM_MOE, FC_MOE)

    fg = final_g.reshape(1, d)
    y_p = _combine(xp, info_p, gt_p, fg, y_sorted, pos[:TOP_K * n_p], TC_COMBINE, t)
    y_s = _combine(xs, info_s, gt_s, fg, y_sorted, pos[TOP_K * n_p:], s, s)

    heads = (N_HEADS, HEAD_DIM)
    return (y_p.reshape(b, t, d), y_s.reshape(s, 1, d),
            k_p.reshape(b, *heads, t).transpose(0, 3, 1, 2),
            v_p.reshape(b, *heads, t).transpose(0, 3, 1, 2),
            k_s.reshape(s, 1, *heads), v_s.reshape(s, 1, *heads),
            sgu_v_s.reshape(s, 1, width))
```

```python
import functools

import jax
import jax.numpy as jnp
from jax import lax
from jax.experimental import pallas as pl
from jax.experimental.pallas import tpu as pltpu

F32 = jnp.float32
BF16 = jnp.bfloat16

N_HEADS = 16
HEAD_DIM = 64
CHUNK = 128
SGU_GROUP_DIM = 128
N_EXPERTS = 8
TOP_K = 2
RMS_EPS = 1e-6
PAGE_SIZE = 128

LANES = 128
VMEM_LIMIT = 56 * 1024 * 1024

TM_QKV = 512
TM_FFN = 512
TM_SGU = 256
TM_ROUTE = 512
TQ_ATTN = 512
TK_ATTN = 256
TM_MOE = 512
FC_MOE = 1792
TC_COMBINE = 512
PAGES_PER_STEP = 8


def _cparams(sem):
    return pltpu.CompilerParams(dimension_semantics=sem, vmem_limit_bytes=VMEM_LIMIT)


def _resident(shape, index_map):
    return pl.BlockSpec(shape, index_map, pipeline_mode=pl.Buffered(1))


def _dot(a, b):
    return jnp.dot(a, b, preferred_element_type=F32)


def _rms(x, g):
    return x * lax.rsqrt(jnp.mean(x * x, axis=-1, keepdims=True) + RMS_EPS) * g


def _modnorm(x, g, scale, shift):
    return _rms(x, g) * (1.0 + scale) + shift


def _sigmoid(x):
    return 1.0 / (1.0 + jnp.exp(-x))


def _silu(x):
    return x * _sigmoid(x)


def _gelu_tanh(x):
    c = 0.7978845608028654
    return 0.5 * x * (1.0 + jnp.tanh(c * (x + 0.044715 * (x * x * x))))


LOG2E = 1.4426950408889634
MASKED_LOG = -1e30


def _neg_abs(z):
    bits = lax.bitcast_convert_type(z, jnp.uint32) | jnp.uint32(0x80000000)
    return lax.bitcast_convert_type(bits, F32)


def _mod_spec(mod, tm, rows_per_group):
    _, r, d = mod.shape
    tiles_per_group = rows_per_group // tm
    return pl.BlockSpec((1, r, d), lambda i: (i // tiles_per_group, 0, 0))


def _ada_kernel(c_ref, w_ref, b_ref, o_ref):
    s = _silu(c_ref[...])
    o_ref[0] = jnp.dot(s, w_ref[0], preferred_element_type=F32,
                       precision=lax.Precision.HIGHEST) + b_ref[0]


def _ada_modulation(c_all, ada_w, ada_b):
    r, d = c_all.shape
    n_mod = ada_w.shape[0] * ada_w.shape[1]
    w = ada_w.reshape(n_mod, d, 3 * d)
    b = ada_b.reshape(n_mod, 1, 3 * d)
    tn = d
    return pl.pallas_call(
        _ada_kernel,
        grid=(n_mod, 3 * d // tn),
        in_specs=[
            pl.BlockSpec((r, d), lambda i, j: (0, 0)),
            pl.BlockSpec((1, d, tn), lambda i, j: (i, 0, j)),
            pl.BlockSpec((1, 1, tn), lambda i, j: (i, 0, j)),
        ],
        out_specs=pl.BlockSpec((1, r, tn), lambda i, j: (i, 0, j)),
        out_shape=jax.ShapeDtypeStruct((n_mod, r, 3 * d), F32),
        compiler_params=_cparams(("arbitrary", "arbitrary")),
    )(c_all, w, b)


def _qkv_kernel(x_ref, g_ref, sc_ref, sh_ref, w_ref, q_ref, kb_ref, vb_ref, kf_ref, vf_ref,
                *, channels_first):
    d = x_ref.shape[1]
    h = _modnorm(x_ref[...], g_ref[...], sc_ref[0], sh_ref[0]).astype(BF16)
    q = _dot(h, w_ref[:, 0:d])
    q_ref[...] = (q * (HEAD_DIM ** -0.5 * LOG2E)).astype(BF16)
    k = _dot(h, w_ref[:, d:2 * d])
    kb_ref[...] = k.astype(BF16)
    v = _dot(h, w_ref[:, 2 * d:3 * d])
    vb_ref[...] = v.astype(BF16)
    if channels_first:
        kf_ref[0] = k.T
        vf_ref[0] = v.T
    else:
        kf_ref[...] = k
        vf_ref[...] = v


def _qkv(x, g, scale, shift, w_qkv_b, tm, rows_per_group, channels_first):
    n, d = x.shape
    row = pl.BlockSpec((tm, d), lambda i: (i, 0))
    if channels_first:
        tiles = rows_per_group // tm
        kv_spec = pl.BlockSpec((1, d, tm), lambda i: (i // tiles, 0, i % tiles))
        kv_shape = jax.ShapeDtypeStruct((n // rows_per_group, d, rows_per_group), F32)
    else:
        kv_spec, kv_shape = row, jax.ShapeDtypeStruct((n, d), F32)
    return pl.pallas_call(
        functools.partial(_qkv_kernel, channels_first=channels_first),
        grid=(n // tm,),
        in_specs=[
            row,
            _resident((1, d), lambda i: (0, 0)),
            _mod_spec(scale, tm, rows_per_group),
            _mod_spec(shift, tm, rows_per_group),
            _resident((d, 3 * d), lambda i: (0, 0)),
        ],
        out_specs=[row, row, row, kv_spec, kv_spec],
        out_shape=[
            jax.ShapeDtypeStruct((n, d), BF16),
            jax.ShapeDtypeStruct((n, d), BF16),
            jax.ShapeDtypeStruct((n, d), BF16),
            kv_shape,
            kv_shape,
        ],
        compiler_params=_cparams(("arbitrary",)),
    )(x, g, scale, shift, w_qkv_b)


def _softplus2(z):
    return jnp.maximum(z, 0.0) + jnp.log2(1.0 + jnp.exp2(_neg_abs(z)))


def _prompt_attn_body(bias_ref, q_ref, k_ref, v_ref, u_ref, o_ref,
                      acc_ref, carry_ref, zs_ref, sp_ref, a_ref, vh_ref):
    tq = q_ref.shape[1]
    tk = u_ref.shape[0]
    assert tq == 2 * tk, "the pipeline below is written for two key tiles per query block"
    hp = pl.program_id(1)
    qi = pl.program_id(2)

    lane = lax.broadcasted_iota(jnp.int32, (1, LANES), 1)
    first = (lane < HEAD_DIM).astype(F32)
    head_masks = (first.astype(BF16), (1.0 - first).astype(BF16))
    q = q_ref[0]
    qhs = [q * m for m in head_masks]
    biases = [bias_ref[2 * hp], bias_ref[2 * hp + 1]]
    neg_from_here = u_ref[...]

    acc_ref[...] = jnp.zeros_like(acc_ref)
    carry_ref[...] = jnp.zeros_like(carry_ref)

    row = lax.broadcasted_iota(jnp.int32, (tq, tk), 0)
    col = lax.broadcasted_iota(jnp.int32, (tq, tk), 1)
    newest_tile = 2 * qi + 1

    def key_start(item):
        return pl.multiple_of((newest_tile - item) * tk, tk)

    def score(item, slot, diag_offset=None):
        kblk = k_ref[0, pl.ds(key_start(item), tk), :]
        for hh in range(2):
            z = lax.dot_general(qhs[hh], kblk, (((1,), (1,)), ((), ())),
                                preferred_element_type=F32) + biases[hh]
            sp = _softplus2(z)
            if diag_offset is not None:
                valid = (col + diag_offset) < row
                sp = jnp.where(valid, sp, 0.0)
                z = jnp.where(valid, z, MASKED_LOG)
            zs_ref[slot, hh] = z
            sp_ref[slot, hh] = sp.astype(BF16)

    def weights(slot):
        for hh in range(2):
            from_here = _dot(sp_ref[slot, hh], neg_from_here)
            c = carry_ref[hh]
            a_ref[slot, hh] = jnp.exp2(zs_ref[slot, hh] + from_here + c).astype(BF16)
            carry_ref[hh] = c + from_here[:, 0:1]

    @pl.when(qi == 0)
    def _():
        for hh in range(2):
            vh_ref[hh] = v_ref[0] * head_masks[hh]

    def values(item, slot):
        keys = pl.ds(key_start(item), tk)
        acc_ref[...] += (_dot(a_ref[slot, 0], vh_ref[0, keys, :])
                         + _dot(a_ref[slot, 1], vh_ref[1, keys, :]))

    score(0, 0, diag_offset=tk)
    score(1, 1, diag_offset=0)
    weights(0)

    def body(jj, c):
        j = 2 + 2 * jj
        score(j, 0)
        weights(1)
        values(j - 2, 0)
        score(j + 1, 1)
        weights(0)
        values(j - 1, 1)
        return c

    lax.fori_loop(0, qi, body, 0)
    n_tiles = 2 + 2 * qi
    weights(1)
    values(n_tiles - 2, 0)
    values(n_tiles - 1, 1)
    o_ref[0] = acc_ref[...].astype(o_ref.dtype)


def _decode_attn_body(p, last_p, bias_ref, q_ref, kn_ref, vn_ref, u_ref, k_refs, v_refs,
                      o_ref, acc_ref, carry_ref):
    neg_from_here = u_ref[...]
    bias = bias_ref[...]
    head_rows = [pl.ds(h * HEAD_DIM, HEAD_DIM) for h in range(N_HEADS)]

    def visit(pages, valid):
        z_rows = [[None] * N_HEADS for _ in pages]
        for h, rows in enumerate(head_rows):
            qh = q_ref[0, rows, :]
            for i, (k_ref, _) in enumerate(pages):
                z_rows[i][h] = jnp.sum(k_ref[0, rows, :] * qh, axis=0, keepdims=True)
        scored = []
        for rows_of_page in z_rows:
            z = jnp.concatenate(rows_of_page, axis=0) + bias
            sp = _softplus2(z)
            if valid is not None:
                sp = jnp.where(valid, sp, 0.0)
                z = jnp.where(valid, z, MASKED_LOG)
            scored.append((z, sp.astype(BF16)))
        sums = [_dot(spb, neg_from_here) for _, spb in scored]
        c = carry_ref[...]
        weights = []
        for (z, _), from_here in zip(scored, sums):
            weights.append(jnp.exp2(z + from_here + c))
            c = c + from_here[:, 0:1]
        carry_ref[...] = c
        for h, rows in enumerate(head_rows):
            part = acc_ref[rows, :]
            for (_, v_ref), a in zip(pages, weights):
                part = part + a[h:h + 1, :] * v_ref[0, rows, :]
            acc_ref[rows, :] = part

    @pl.when(p == 0)
    def _():
        acc_ref[...] = jnp.zeros_like(acc_ref)
        carry_ref[...] = jnp.zeros_like(carry_ref)
        n_new = 1
        q_index = 0
        key_pos = lax.broadcasted_iota(jnp.int32, (N_HEADS, PAGE_SIZE), 1)
        visit([(kn_ref, vn_ref)], (key_pos < n_new) & (key_pos < q_index))

    visit(list(zip(k_refs, v_refs)), None)

    @pl.when(p == last_p)
    def _():
        ones = jnp.ones((8, PAGE_SIZE), F32)
        total = lax.dot_general(ones, acc_ref[...], (((1,), (1,)), ((), ())),
                                preferred_element_type=F32,
                                precision=lax.Precision.HIGHEST)
        o_ref[0] = total[0:1, :]


def _attention_kernel(pt_ref, bias_ref, q_ref, k_ref, v_ref, u_ref,
                      dbias_ref, dq_ref, kn_ref, vn_ref, du_ref, *rest, steps_per_seq):
    npg = PAGES_PER_STEP
    k_pages = rest[0:npg]
    v_pages = rest[npg:2 * npg]
    o_ref, do_ref = rest[2 * npg:2 * npg + 2]
    (acc_ref, carry_ref, zs_ref, sp_ref, a_ref, vh_ref,
     dacc_ref, dcarry_ref) = rest[2 * npg + 2:]
    step = ((pl.program_id(0) * pl.num_programs(1) + pl.program_id(1)) * pl.num_programs(2)
            + pl.program_id(2))
    _decode_attn_body(step % steps_per_seq, steps_per_seq - 1, dbias_ref, dq_ref, kn_ref,
                      vn_ref, du_ref, k_pages, v_pages, do_ref, dacc_ref, dcarry_ref)
    _prompt_attn_body(bias_ref, q_ref, k_ref, v_ref, u_ref, o_ref,
                      acc_ref, carry_ref, zs_ref, sp_ref, a_ref, vh_ref)


def _attention(q, k, v, bias2, u_tile, tq, dq, k_new, v_new, cache_k, cache_v, page_table,
               u_page):
    b, t, d = q.shape
    tk = u_tile.shape[0]
    n_pairs = d // LANES
    nq = t // tq
    s = dq.shape[0]
    n_pages = page_table.shape[1]
    npg = PAGES_PER_STEP
    steps_per_seq = n_pages // npg
    assert n_pages % npg == 0 and s * steps_per_seq == b * n_pairs * nq, (
        "the decode sweep is spread over exactly the prompt sweep's grid steps")
    pt_flat = page_table.reshape(-1)

    def linear(bi, hp, qi):
        return (bi * n_pairs + hp) * nq + qi

    def seq_map(bi, hp, qi, pt):
        return (linear(bi, hp, qi) // steps_per_seq, 0, 0)

    def page_spec(i):
        def index_map(bi, hp, qi, pt):
            step = linear(bi, hp, qi)
            si, p = step // steps_per_seq, step % steps_per_seq
            return (pt[si * n_pages + (n_pages - 1 - (p * npg + i))], 0, 0)
        return pl.BlockSpec((1, d, PAGE_SIZE), index_map)

    fixed = lambda bi, hp, qi, pt: (0, 0)
    per_seq = pl.BlockSpec((1, 1, d), seq_map)
    new_page = pl.BlockSpec((1, d, PAGE_SIZE), seq_map)
    q_block = pl.BlockSpec((1, tq, LANES), lambda bi, hp, qi, pt: (bi, qi, hp))
    kv_block = pl.BlockSpec((1, t, LANES), lambda bi, hp, qi, pt: (bi, 0, hp))
    grid_spec = pltpu.PrefetchScalarGridSpec(
        num_scalar_prefetch=1,
        grid=(b, n_pairs, nq),
        in_specs=[
            pl.BlockSpec(memory_space=pltpu.SMEM),
            q_block, kv_block, kv_block,
            _resident(u_tile.shape, fixed),
            _resident((N_HEADS, 1), fixed),
            new_page, new_page, new_page,
            _resident(u_page.shape, fixed),
        ] + [page_spec(i) for i in range(npg)] + [page_spec(i) for i in range(npg)],
        out_specs=[q_block, per_seq],
        scratch_shapes=[
            pltpu.VMEM((tq, LANES), F32),
            pltpu.VMEM((2, tq, 1), F32),
            pltpu.VMEM((2, 2, tq, tk), F32),
            pltpu.VMEM((2, 2, tq, tk), BF16),
            pltpu.VMEM((2, 2, tq, tk), BF16),
            pltpu.VMEM((2, t, LANES), BF16),
            pltpu.VMEM((d, PAGE_SIZE), F32),
            pltpu.VMEM((N_HEADS, 1), F32),
        ],
    )
    return pl.pallas_call(
        functools.partial(_attention_kernel, steps_per_seq=steps_per_seq),
        grid_spec=grid_spec,
        out_shape=[jax.ShapeDtypeStruct((b, t, d), BF16),
                   jax.ShapeDtypeStruct((s, 1, d), F32)],
        compiler_params=_cparams(("arbitrary", "arbitrary", "arbitrary")),
    )(pt_flat, bias2, q, k, v, u_tile, bias2.reshape(N_HEADS, 1), dq, k_new, v_new, u_page,
      *([cache_k] * npg), *([cache_v] * npg))


def _ffn_kernel(o_ref, x_ref, gt0_ref, wo_ref, g_ref, sc_ref, sh_ref, gt1_ref,
                wg_ref, wu_ref, wd_ref, out_ref, *, ff_chunk):
    x1 = x_ref[...] + gt0_ref[0] * _dot(o_ref[...], wo_ref[...])
    h = _modnorm(x1, g_ref[...], sc_ref[0], sh_ref[0]).astype(BF16)
    d_ff = wg_ref.shape[1]
    f = None
    for c0 in range(0, d_ff, ff_chunk):
        hg = _dot(h, wg_ref[:, c0:c0 + ff_chunk])
        hu = _dot(h, wu_ref[:, c0:c0 + ff_chunk])
        part = _dot((_silu(hg) * hu).astype(BF16), wd_ref[c0:c0 + ff_chunk, :])
        f = part if f is None else f + part
    out_ref[...] = x1 + gt1_ref[0] * f


def _attn_out_ffn(o, x, gate0, w_o_b, g, scale, shift, gate1, wg_b, wu_b, wd_b,
                  tm, rows_per_group):
    n, d = x.shape
    d_ff = wg_b.shape[1]
    row = pl.BlockSpec((tm, d), lambda i: (i, 0))
    fixed = lambda i: (0, 0)
    mod = lambda m: _mod_spec(m, tm, rows_per_group)
    return pl.pallas_call(
        functools.partial(_ffn_kernel, ff_chunk=d_ff // 2),
        grid=(n // tm,),
        in_specs=[
            row, row, mod(gate0), _resident((d, d), fixed),
            _resident((1, d), fixed), mod(scale), mod(shift), mod(gate1),
            _resident((d, d_ff), fixed), _resident((d, d_ff), fixed),
            _resident((d_ff, d), fixed),
        ],
        out_specs=row,
        out_shape=jax.ShapeDtypeStruct((n, d), F32),
        compiler_params=_cparams(("arbitrary",)),
    )(o, x, gate0, w_o_b, g, scale, shift, gate1, wg_b, wu_b, wd_b)


def _sgu_uv(x_ref, g_ref, sc_ref, sh_ref, win_ref, gv_ref):
    width = gv_ref.shape[1]
    h = _modnorm(x_ref[...], g_ref[...], sc_ref[0], sh_ref[0]).astype(BF16)
    u = _gelu_tanh(_dot(h, win_ref[:, 0:width]))
    v = _rms(_gelu_tanh(_dot(h, win_ref[:, width:2 * width])), gv_ref[...])
    return u, v


def _sgu_prompt_kernel(x_ref, g_ref, sc_ref, sh_ref, gt_ref, win_ref, gv_ref,
                       ws_ref, bs_ref, wout_ref, out_ref, mixed_ref):
    tm = x_ref.shape[0]
    n_groups = ws_ref.shape[0]
    u, v = _sgu_uv(x_ref, g_ref, sc_ref, sh_ref, win_ref, gv_ref)
    vb = v.astype(BF16)
    i_pos = lax.broadcasted_iota(jnp.int32, (CHUNK, CHUNK), 0)
    j_pos = lax.broadcasted_iota(jnp.int32, (CHUNK, CHUNK), 1)
    causal = j_pos <= i_pos
    for gi in range(n_groups):
        ws = jnp.where(causal, ws_ref[gi], 0.0).astype(BF16)
        c0 = gi * SGU_GROUP_DIM
        for ci in range(tm // CHUNK):
            r0 = ci * CHUNK
            mixed_ref[r0:r0 + CHUNK, c0:c0 + SGU_GROUP_DIM] = _dot(
                ws, vb[r0:r0 + CHUNK, c0:c0 + SGU_GROUP_DIM])
    bias = bs_ref[...]
    for ci in range(tm // CHUNK):
        r0 = ci * CHUNK
        mixed_ref[r0:r0 + CHUNK, :] += bias
    y = _dot((u * mixed_ref[...]).astype(BF16), wout_ref[...])
    out_ref[...] = x_ref[...] + gt_ref[0] * y


def _sgu_prompt(x, g, scale, shift, gate, win_b, g_v, w_s, bias_rows, wout_b,
                tm, rows_per_group):
    n, d = x.shape
    width = g_v.shape[1]
    row = pl.BlockSpec((tm, d), lambda i: (i, 0))
    fixed = lambda i: (0, 0)
    mod = lambda m: _mod_spec(m, tm, rows_per_group)
    return pl.pallas_call(
        _sgu_prompt_kernel,
        grid=(n // tm,),
        in_specs=[
            row, _resident((1, d), fixed), mod(scale), mod(shift), mod(gate),
            _resident((d, 2 * width), fixed), _resident((1, width), fixed),
            _resident(w_s.shape, lambda i: (0, 0, 0)),
            _resident(bias_rows.shape, fixed), _resident((width, d), fixed),
        ],
        out_specs=row,
        out_shape=jax.ShapeDtypeStruct((n, d), F32),
        scratch_shapes=[pltpu.VMEM((tm, width), F32)],
        compiler_params=_cparams(("arbitrary",)),
    )(x, g, scale, shift, gate, win_b, g_v, w_s, bias_rows, wout_b)


def _sgu_decode_kernel(x_ref, g_ref, sc_ref, sh_ref, gt_ref, win_ref, gv_ref,
                       ws0_ref, bs0_ref, wout_ref, out_ref, v_ref):
    u, v = _sgu_uv(x_ref, g_ref, sc_ref, sh_ref, win_ref, gv_ref)
    v_ref[...] = v
    mixed = ws0_ref[...] * v + bs0_ref[...]
    y = _dot((u * mixed).astype(BF16), wout_ref[...])
    out_ref[...] = x_ref[...] + gt_ref[0] * y


def _sgu_decode(x, g, scale, shift, gate, win_b, g_v, ws0_row, bs0_row, wout_b):
    n, d = x.shape
    width = g_v.shape[1]
    whole = lambda a: pl.BlockSpec(a.shape, lambda i: (0,) * a.ndim)
    args = (x, g, scale, shift, gate, win_b, g_v, ws0_row, bs0_row, wout_b)
    return pl.pallas_call(
        _sgu_decode_kernel,
        grid=(1,),
        in_specs=[whole(a) for a in args],
        out_specs=[pl.BlockSpec((n, d), lambda i: (0, 0)),
                   pl.BlockSpec((n, width), lambda i: (0, 0))],
        out_shape=[jax.ShapeDtypeStruct((n, d), F32),
                   jax.ShapeDtypeStruct((n, width), F32)],
        compiler_params=_cparams(("arbitrary",)),
    )(*args)


def _route_kernel(x_ref, g_ref, sc_ref, sh_ref, wr_ref, h_ref, info_ref):
    h = _modnorm(x_ref[...], g_ref[...], sc_ref[0], sh_ref[0])
    h_ref[...] = h
    logits = jnp.dot(h, wr_ref[...], preferred_element_type=F32,
                     precision=lax.Precision.HIGHEST)
    lane = lax.broadcasted_iota(jnp.int32, logits.shape, 1)
    lane_f = lane.astype(F32)
    neg = jnp.float32(-jnp.inf)
    l1 = jnp.where(lane < N_EXPERTS, logits, neg)
    m1 = jnp.max(l1, axis=-1, keepdims=True)
    i1 = jnp.min(jnp.where(l1 == m1, lane_f, float(LANES)), axis=-1, keepdims=True)
    l2 = jnp.where(lane_f == i1, neg, l1)
    m2 = jnp.max(l2, axis=-1, keepdims=True)
    i2 = jnp.min(jnp.where(l2 == m2, lane_f, float(LANES)), axis=-1, keepdims=True)
    e = jnp.exp(m2 - m1)
    g1 = 1.0 / (1.0 + e)
    g2 = e / (1.0 + e)
    info = jnp.where(lane == 0, g1, 0.0)
    info = jnp.where(lane == 1, g2, info)
    info = jnp.where(lane == 2, i1, info)
    info = jnp.where(lane == 3, i2, info)
    info_ref[...] = info


def _route(x, g, scale, shift, w_router_pad, tm, rows_per_group):
    n, d = x.shape
    row = pl.BlockSpec((tm, d), lambda i: (i, 0))
    fixed = lambda i: (0, 0)
    return pl.pallas_call(
        _route_kernel,
        grid=(n // tm,),
        in_specs=[
            row, _resident((1, d), fixed),
            _mod_spec(scale, tm, rows_per_group), _mod_spec(shift, tm, rows_per_group),
            _resident((d, LANES), fixed),
        ],
        out_specs=[row, pl.BlockSpec((tm, LANES), lambda i: (i, 0))],
        out_shape=[jax.ShapeDtypeStruct((n, d), F32),
                   jax.ShapeDtypeStruct((n, LANES), F32)],
        compiler_params=_cparams(("arbitrary",)),
    )(x, g, scale, shift, w_router_pad)


def _scatter_rows_kernel(pos_ref, src_ref, *rest, tm, zero_fill):
    if zero_fill:
        out_ref, zero_ref, sem = rest
    else:
        _, out_ref, sem = rest
    i = pl.program_id(0)
    n_copies = TOP_K * tm

    if zero_fill:
        @pl.when(i == 0)
        def _():
            fill = zero_ref.shape[0]
            zero_ref[...] = jnp.zeros_like(zero_ref)

            def fill_copy(j):
                return pltpu.make_async_copy(
                    zero_ref, out_ref.at[pl.ds(pl.multiple_of(j * fill, fill), fill)], sem)

            def start(j, c):
                fill_copy(j).start()
                return c

            def wait(j, c):
                fill_copy(j).wait()
                return c

            lax.fori_loop(0, out_ref.shape[0] // fill, start, 0)
            lax.fori_loop(0, out_ref.shape[0] // fill, wait, 0)

    def issue(r, c):
        for k in range(TOP_K):
            pltpu.make_async_copy(src_ref.at[pl.ds(r, 1)],
                                  out_ref.at[pl.ds(pos_ref[0, 0, TOP_K * r + k], 1)],
                                  sem).start()
        return c

    lax.fori_loop(0, tm, issue, 0, unroll=8)
    pltpu.make_async_copy(out_ref.at[pl.ds(0, n_copies)],
                          out_ref.at[pl.ds(n_copies, n_copies)], sem).wait()


def _scatter_rows(src, pos, tm, n_out=None, dst=None):
    n, d = src.shape
    n_tiles = n // tm
    zero_fill = dst is None
    n_out = n_out if zero_fill else dst.shape[0]
    any_spec = pl.BlockSpec(memory_space=pl.ANY)
    pos_spec = pl.BlockSpec((1, 1, TOP_K * tm), lambda i: (i, 0, 0), memory_space=pltpu.SMEM)
    scratch = [pltpu.SemaphoreType.DMA(())]
    if zero_fill:
        scratch = [pltpu.VMEM((TM_MOE, d), src.dtype)] + scratch
    return pl.pallas_call(
        functools.partial(_scatter_rows_kernel, tm=tm, zero_fill=zero_fill),
        grid=(n_tiles,),
        in_specs=[pos_spec, pl.BlockSpec((tm, d), lambda i: (i, 0))]
        + ([] if zero_fill else [any_spec]),
        out_specs=any_spec,
        out_shape=jax.ShapeDtypeStruct((n_out, d), src.dtype),
        scratch_shapes=scratch,
        input_output_aliases={} if zero_fill else {2: 0},
        compiler_params=_cparams(("arbitrary",)),
    )(pos.reshape(n_tiles, 1, TOP_K * tm), src, *([] if zero_fill else [dst]))


def _expert_kernel(te_ref, nu_ref, xs_ref, wg_ref, wu_ref, wd_ref, out_ref,
                   xb_ref, acc_ref):
    i = pl.program_id(0)
    c = pl.program_id(1)

    @pl.when(i < nu_ref[0])
    def _():
        @pl.when(c == 0)
        def _():
            xb_ref[...] = xs_ref[...].astype(BF16)
            acc_ref[...] = jnp.zeros_like(acc_ref)

        xb = xb_ref[...]
        hg = _dot(xb, wg_ref[0])
        hu = _dot(xb, wu_ref[0])
        acc_ref[...] += _dot((_silu(hg) * hu).astype(BF16), wd_ref[0])

        @pl.when(c == pl.num_programs(1) - 1)
        def _():
            out_ref[...] = acc_ref[...]

    @pl.when((i >= nu_ref[0]) & (c == pl.num_programs(1) - 1))
    def _():
        out_ref[...] = jnp.zeros_like(out_ref)


def _expert_swiglu(xs, tile_expert, n_used, wg_b, wu_b, wd_b, tm, fc):
    n_rows, d = xs.shape
    d_ff = wg_b.shape[2]
    n_tiles = n_rows // tm
    n_chunks = d_ff // fc

    def live(i, nu):
        return jnp.minimum(i, nu[0] - 1)

    def chunk(i, c, nu):
        return jnp.where(i < nu[0], c, n_chunks - 1)

    grid_spec = pltpu.PrefetchScalarGridSpec(
        num_scalar_prefetch=2,
        grid=(n_tiles, n_chunks),
        in_specs=[
            pl.BlockSpec((tm, d), lambda i, c, te, nu: (live(i, nu), 0)),
            pl.BlockSpec((1, d, fc), lambda i, c, te, nu: (te[live(i, nu)], 0, chunk(i, c, nu))),
            pl.BlockSpec((1, d, fc), lambda i, c, te, nu: (te[live(i, nu)], 0, chunk(i, c, nu))),
            pl.BlockSpec((1, fc, d), lambda i, c, te, nu: (te[live(i, nu)], chunk(i, c, nu), 0)),
        ],
        out_specs=pl.BlockSpec((tm, d), lambda i, c, te, nu: (i, 0)),
        scratch_shapes=[pltpu.VMEM((tm, d), BF16), pltpu.VMEM((tm, d), F32)],
    )
    return pl.pallas_call(
        _expert_kernel,
        grid_spec=grid_spec,
        out_shape=jax.ShapeDtypeStruct((n_rows, d), F32),
        compiler_params=_cparams(("arbitrary", "arbitrary")),
    )(tile_expert, n_used, xs, wg_b, wu_b, wd_b)


def _combine_kernel(pos_ref, x_ref, info_ref, gt_ref, gf_ref, y_ref, out_ref, ybuf, sem):
    tc = x_ref.shape[0]

    def issue(r, c):
        for k in range(TOP_K):
            pltpu.make_async_copy(y_ref.at[pl.ds(pos_ref[0, 0, TOP_K * r + k], 1)],
                                  ybuf.at[k, pl.ds(r, 1)], sem).start()
        return c

    lax.fori_loop(0, tc, issue, 0, unroll=min(8, tc))
    for k in range(TOP_K):
        pltpu.make_async_copy(y_ref.at[pl.ds(0, tc)], ybuf.at[k], sem).wait()
    info = info_ref[...]
    moe = info[:, 0:1] * ybuf[0] + info[:, 1:2] * ybuf[1]
    x = x_ref[...] + gt_ref[0] * moe
    out_ref[...] = _rms(x, gf_ref[...])


def _combine(x, info, gate, final_g, y_sorted, pos, tc, rows_per_group):
    n, d = x.shape
    n_tiles = n // tc
    return pl.pallas_call(
        _combine_kernel,
        grid=(n_tiles,),
        in_specs=[
            pl.BlockSpec((1, 1, TOP_K * tc), lambda i: (i, 0, 0), memory_space=pltpu.SMEM),
            pl.BlockSpec((tc, d), lambda i: (i, 0)),
            pl.BlockSpec((tc, LANES), lambda i: (i, 0)),
            _mod_spec(gate, tc, rows_per_group),
            _resident((1, d), lambda i: (0, 0)),
            pl.BlockSpec(memory_space=pl.ANY),
        ],
        out_specs=pl.BlockSpec((tc, d), lambda i: (i, 0)),
        out_shape=jax.ShapeDtypeStruct((n, d), F32),
        scratch_shapes=[pltpu.VMEM((TOP_K, tc, d), F32), pltpu.SemaphoreType.DMA(())],
        compiler_params=_cparams(("arbitrary",)),
    )(pos.reshape(n_tiles, 1, TOP_K * tc), x, info, gate, final_g, y_sorted)


def _expert_layout(experts, tm):
    n_assign = experts.shape[0]
    n_tiles = n_assign // tm + N_EXPERTS
    onehot = (experts[:, None] == jnp.arange(N_EXPERTS, dtype=jnp.int32)[None, :]).astype(jnp.int32)
    running = jnp.cumsum(onehot, axis=0)
    counts = running[-1]
    rank = jnp.sum(running * onehot, axis=1) - 1
    tiles_per_expert = (counts + tm - 1) // tm
    tile_end = jnp.cumsum(tiles_per_expert)
    row_start = (tile_end - tiles_per_expert) * tm
    pos = row_start[experts] + rank
    n_used = tile_end[-1:]
    tile_ids = jnp.arange(n_tiles, dtype=jnp.int32)
    tile_expert = jnp.minimum(
        jnp.sum((tile_ids[:, None] >= tile_end[None, :]).astype(jnp.int32), axis=1),
        N_EXPERTS - 1).astype(jnp.int32)
    return pos.astype(jnp.int32), tile_expert, n_used.astype(jnp.int32), n_tiles


def kernel(x_prompt, x_sample, cache_k, cache_v, page_table, c_prompt, c_sample, norm_g, ada_w, ada_b, final_g, attn_w_qkv, attn_w_o, attn_b_score, sgu_w_in, sgu_g_v, sgu_w_s, sgu_b_s, sgu_w_out, ffn_w_gate, ffn_w_up, ffn_w_down, moe_w_router, moe_w_gate, moe_w_up, moe_w_down):
    b, t, d = x_prompt.shape
    s, t_new, _ = x_sample.shape
    assert t_new == 1, "the sample group decodes one token per sequence"
    n_p = b * t

    n_c = b + s
    n_c_pad = -(-n_c // 8) * 8
    c_all = jnp.concatenate(
        [c_prompt, c_sample, jnp.zeros((n_c_pad - n_c, d), F32)], axis=0)
    mods = _ada_modulation(c_all, ada_w, ada_b)

    def mod_parts(layer, sub):
        m = mods[2 * layer + sub]
        parts = []
        for j in range(3):
            col = m[:, j * d:(j + 1) * d]
            parts.append((col[0:b].reshape(b, 1, d), col[b:b + s].reshape(1, s, d)))
        return parts

    bf = lambda w: w.astype(BF16)
    g_row = lambda layer, sub: norm_g[layer, sub].reshape(1, d)
    xp = x_prompt.reshape(n_p, d)
    xs = x_sample.reshape(s, d)

    (sh_p, sh_s), (sc_p, sc_s), (gt0_p, gt0_s) = mod_parts(0, 0)
    w_qkv_b = bf(attn_w_qkv)
    q_p, kb_p, vb_p, k_p, v_p = _qkv(xp, g_row(0, 0), sc_p, sh_p, w_qkv_b, TM_QKV, t, True)
    q_s, kb_s, vb_s, k_s, v_s = _qkv(xs, g_row(0, 0), sc_s, sh_s, w_qkv_b, s, s, False)

    def neg_upper(n):
        i = jnp.arange(n, dtype=jnp.int32)
        return -(i[:, None] >= i[None, :]).astype(BF16)

    bias2 = attn_b_score * LOG2E
    keys_last = lambda c: c.transpose(0, 2, 3, 1).reshape(c.shape[0], d, PAGE_SIZE)

    def new_page(a):
        return jnp.pad(a[:, :, None], ((0, 0), (0, 0), (0, PAGE_SIZE - 1)))

    o_p, o_s = _attention(
        q_p.reshape(b, t, d), kb_p.reshape(b, t, d), vb_p.reshape(b, t, d), bias2,
        neg_upper(TK_ATTN), TQ_ATTN,
        jnp.broadcast_to(q_s.astype(F32)[:, :, None], (s, d, PAGE_SIZE)),
        new_page(k_s), new_page(v_s),
        keys_last(cache_k), keys_last(cache_v), page_table, neg_upper(PAGE_SIZE))
    o_s = o_s.astype(BF16)

    (sh_p, sh_s), (sc_p, sc_s), (gt1_p, gt1_s) = mod_parts(0, 1)
    w_o_b, wg_b, wu_b, wd_b = bf(attn_w_o), bf(ffn_w_gate), bf(ffn_w_up), bf(ffn_w_down)
    xp = _attn_out_ffn(o_p.reshape(n_p, d), xp, gt0_p, w_o_b, g_row(0, 1), sc_p, sh_p,
                       gt1_p, wg_b, wu_b, wd_b, TM_FFN, t)
    xs = _attn_out_ffn(o_s.reshape(s, d), xs, gt0_s, w_o_b, g_row(0, 1), sc_s, sh_s,
                       gt1_s, wg_b, wu_b, wd_b, s, s)

    (sh_p, sh_s), (sc_p, sc_s), (gt_p, gt_s) = mod_parts(1, 0)
    width = sgu_g_v.shape[0]
    n_groups = sgu_w_s.shape[0]
    win_b, wout_b = bf(sgu_w_in), bf(sgu_w_out)
    g_v = sgu_g_v.reshape(1, width)
    bias_rows = jnp.broadcast_to(sgu_b_s.T[:, :, None],
                                 (CHUNK, n_groups, SGU_GROUP_DIM)).reshape(CHUNK, width)
    xp = _sgu_prompt(xp, g_row(1, 0), sc_p, sh_p, gt_p, win_b, g_v, sgu_w_s,
                     bias_rows, wout_b, TM_SGU, t)
    ws0_row = jnp.broadcast_to(sgu_w_s[:, 0, 0][:, None],
                               (n_groups, SGU_GROUP_DIM)).reshape(1, width)
    xs, sgu_v_s = _sgu_decode(xs, g_row(1, 0), sc_s, sh_s, gt_s, win_b, g_v,
                              ws0_row, bias_rows[0:1], wout_b)

    (sh_p, sh_s), (sc_p, sc_s), (gt_p, gt_s) = mod_parts(1, 1)
    wr_pad = jnp.concatenate(
        [moe_w_router, jnp.zeros((d, LANES - N_EXPERTS), F32)], axis=1)
    h_p, info_p = _route(xp, g_row(1, 1), sc_p, sh_p, wr_pad, TM_ROUTE, t)
    h_s, info_s = _route(xs, g_row(1, 1), sc_s, sh_s, wr_pad, s, s)

    experts = jnp.concatenate([info_p[:, 2:2 + TOP_K], info_s[:, 2:2 + TOP_K]],
                              axis=0).astype(jnp.int32).reshape(-1)
    pos, tile_expert, n_used, n_tiles = _expert_layout(experts, TM_MOE)
    x_sorted = _scatter_rows(h_p, pos[:TOP_K * n_p], TM_MOE, n_out=n_tiles * TM_MOE)
    x_sorted = _scatter_rows(h_s, pos[TOP_K * n_p:], s, dst=x_sorted)
    y_sorted = _expert_swiglu(x_sorted, tile_expert, n_used, bf(moe_w_gate),
                              bf(moe_w_up), bf(moe_w_down), TM_MOE, FC_MOE)

    fg = final_g.reshape(1, d)
    y_p = _combine(xp, info_p, gt_p, fg, y_sorted, pos[:TOP_K * n_p], TC_COMBINE, t)
    y_s = _combine(xs, info_s, gt_s, fg, y_sorted, pos[TOP_K * n_p:], s, s)

    heads = (N_HEADS, HEAD_DIM)
    return (y_p.reshape(b, t, d), y_s.reshape(s, 1, d),
            k_p.reshape(b, *heads, t).transpose(0, 3, 1, 2),
            v_p.reshape(b, *heads, t).transpose(0, 3, 1, 2),
            k_s.reshape(s, 1, *heads), v_s.reshape(s, 1, *heads),
            sgu_v_s.reshape(s, 1, width))
```

```python
import functools

import jax
import jax.numpy as jnp
from jax import lax
from jax.experimental import pallas as pl
from jax.experimental.pallas import tpu as pltpu

F32 = jnp.float32
BF16 = jnp.bfloat16

N_HEADS = 16
HEAD_DIM = 64
CHUNK = 128
SGU_GROUP_DIM = 128
N_EXPERTS = 8
TOP_K = 2
RMS_EPS = 1e-6
PAGE_SIZE = 128

LANES = 128
SUBLANES = 8
ROW_TILE = (SUBLANES, LANES)
VMEM_LIMIT = 56 * 1024 * 1024

TM_QKV = 512
TM_FFN = 512
TM_SGU = 256
TM_ROUTE = 512
TQ_ATTN = 512
TK_ATTN = 256
TM_MOE = 512
FC_MOE = 1792
TC_COMBINE = 512
PAGES_PER_STEP = 8


def _cparams(sem):
    return pltpu.CompilerParams(dimension_semantics=sem, vmem_limit_bytes=VMEM_LIMIT)


def _resident(shape, index_map):
    return pl.BlockSpec(shape, index_map, pipeline_mode=pl.Buffered(1))


def _dot(a, b):
    return jnp.dot(a, b, preferred_element_type=F32)


def _rms(x, g):
    return x * lax.rsqrt(jnp.mean(x * x, axis=-1, keepdims=True) + RMS_EPS) * g


def _modnorm(x, g, scale, shift):
    return _rms(x, g) * (1.0 + scale) + shift


def _sigmoid(x):
    return 1.0 / (1.0 + jnp.exp(-x))


def _silu(x):
    return x * _sigmoid(x)


def _gelu_tanh(x):
    c = 0.7978845608028654
    return 0.5 * x * (1.0 + jnp.tanh(c * (x + 0.044715 * (x * x * x))))


LOG2E = 1.4426950408889634
MASKED_LOG = -1e30


def _neg_abs(z):
    bits = lax.bitcast_convert_type(z, jnp.uint32) | jnp.uint32(0x80000000)
    return lax.bitcast_convert_type(bits, F32)


def _mod_spec(mod, tm, rows_per_group):
    _, r, d = mod.shape
    tiles_per_group = rows_per_group // tm
    return pl.BlockSpec((1, r, d), lambda i: (i // tiles_per_group, 0, 0))


def _ada_kernel(c_ref, w_ref, b_ref, o_ref):
    s = _silu(c_ref[...])
    o_ref[0] = jnp.dot(s, w_ref[0], preferred_element_type=F32,
                       precision=lax.Precision.HIGHEST) + b_ref[0]


def _ada_modulation(c_all, ada_w, ada_b):
    r, d = c_all.shape
    n_mod = ada_w.shape[0] * ada_w.shape[1]
    w = ada_w.reshape(n_mod, d, 3 * d)
    b = ada_b.reshape(n_mod, 1, 3 * d)
    tn = d
    return pl.pallas_call(
        _ada_kernel,
        grid=(n_mod, 3 * d // tn),
        in_specs=[
            pl.BlockSpec((r, d), lambda i, j: (0, 0)),
            pl.BlockSpec((1, d, tn), lambda i, j: (i, 0, j)),
            pl.BlockSpec((1, 1, tn), lambda i, j: (i, 0, j)),
        ],
        out_specs=pl.BlockSpec((1, r, tn), lambda i, j: (i, 0, j)),
        out_shape=jax.ShapeDtypeStruct((n_mod, r, 3 * d), F32),
        compiler_params=_cparams(("arbitrary", "arbitrary")),
    )(c_all, w, b)


def _qkv_kernel(x_ref, g_ref, sc_ref, sh_ref, w_ref, q_ref, kb_ref, vb_ref, kf_ref, vf_ref,
                *, channels_first):
    d = x_ref.shape[1]
    h = _modnorm(x_ref[...], g_ref[...], sc_ref[0], sh_ref[0]).astype(BF16)
    q = _dot(h, w_ref[:, 0:d])
    q_ref[...] = (q * (HEAD_DIM ** -0.5 * LOG2E)).astype(BF16)
    k = _dot(h, w_ref[:, d:2 * d])
    kb_ref[...] = k.astype(BF16)
    v = _dot(h, w_ref[:, 2 * d:3 * d])
    vb_ref[...] = v.astype(BF16)
    if channels_first:
        kf_ref[0] = k.T
        vf_ref[0] = v.T
    else:
        kf_ref[...] = k
        vf_ref[...] = v


def _qkv(x, g, scale, shift, w_qkv_b, tm, rows_per_group, channels_first):
    n, d = x.shape
    row = pl.BlockSpec((tm, d), lambda i: (i, 0))
    if channels_first:
        tiles = rows_per_group // tm
        kv_spec = pl.BlockSpec((1, d, tm), lambda i: (i // tiles, 0, i % tiles))
        kv_shape = jax.ShapeDtypeStruct((n // rows_per_group, d, rows_per_group), F32)
    else:
        kv_spec, kv_shape = row, jax.ShapeDtypeStruct((n, d), F32)
    return pl.pallas_call(
        functools.partial(_qkv_kernel, channels_first=channels_first),
        grid=(n // tm,),
        in_specs=[
            row,
            _resident((1, d), lambda i: (0, 0)),
            _mod_spec(scale, tm, rows_per_group),
            _mod_spec(shift, tm, rows_per_group),
            _resident((d, 3 * d), lambda i: (0, 0)),
        ],
        out_specs=[row, row, row, kv_spec, kv_spec],
        out_shape=[
            jax.ShapeDtypeStruct((n, d), BF16),
            jax.ShapeDtypeStruct((n, d), BF16),
            jax.ShapeDtypeStruct((n, d), BF16),
            kv_shape,
            kv_shape,
        ],
        compiler_params=_cparams(("arbitrary",)),
    )(x, g, scale, shift, w_qkv_b)


def _softplus2(z):
    return jnp.maximum(z, 0.0) + jnp.log2(1.0 + jnp.exp2(_neg_abs(z)))


def _prompt_attn_body(bias_ref, q_ref, k_ref, v_ref, u_ref, o_ref,
                      acc_ref, carry_ref, zs_ref, sp_ref, a_ref, vh_ref):
    tq = q_ref.shape[1]
    tk = u_ref.shape[0]
    assert tq == 2 * tk, "the pipeline below is written for two key tiles per query block"
    hp = pl.program_id(1)
    qi = pl.program_id(2)

    lane = lax.broadcasted_iota(jnp.int32, (1, LANES), 1)
    first = (lane < HEAD_DIM).astype(F32)
    head_masks = (first.astype(BF16), (1.0 - first).astype(BF16))
    q = q_ref[0]
    qhs = [q * m for m in head_masks]
    biases = [bias_ref[2 * hp], bias_ref[2 * hp + 1]]
    neg_from_here = u_ref[...]

    acc_ref[...] = jnp.zeros_like(acc_ref)
    carry_ref[...] = jnp.zeros_like(carry_ref)

    row = lax.broadcasted_iota(jnp.int32, (tq, tk), 0)
    col = lax.broadcasted_iota(jnp.int32, (tq, tk), 1)
    newest_tile = 2 * qi + 1

    def key_start(item):
        return pl.multiple_of((newest_tile - item) * tk, tk)

    def score(item, slot, diag_offset=None):
        kblk = k_ref[0, pl.ds(key_start(item), tk), :]
        for hh in range(2):
            z = lax.dot_general(qhs[hh], kblk, (((1,), (1,)), ((), ())),
                                preferred_element_type=F32) + biases[hh]
            sp = _softplus2(z)
            if diag_offset is not None:
                valid = (col + diag_offset) < row
                sp = jnp.where(valid, sp, 0.0)
                z = jnp.where(valid, z, MASKED_LOG)
            zs_ref[slot, hh] = z
            sp_ref[slot, hh] = sp.astype(BF16)

    def weights(slot):
        for hh in range(2):
            from_here = _dot(sp_ref[slot, hh], neg_from_here)
            c = carry_ref[hh]
            a_ref[slot, hh] = jnp.exp2(zs_ref[slot, hh] + from_here + c).astype(BF16)
            carry_ref[hh] = c + from_here[:, 0:1]

    @pl.when(qi == 0)
    def _():
        for hh in range(2):
            vh_ref[hh] = v_ref[0] * head_masks[hh]

    def values(item, slot):
        keys = pl.ds(key_start(item), tk)
        acc_ref[...] += (_dot(a_ref[slot, 0], vh_ref[0, keys, :])
                         + _dot(a_ref[slot, 1], vh_ref[1, keys, :]))

    score(0, 0, diag_offset=tk)
    score(1, 1, diag_offset=0)
    weights(0)

    def body(jj, c):
        j = 2 + 2 * jj
        score(j, 0)
        weights(1)
        values(j - 2, 0)
        score(j + 1, 1)
        weights(0)
        values(j - 1, 1)
        return c

    lax.fori_loop(0, qi, body, 0)
    n_tiles = 2 + 2 * qi
    weights(1)
    values(n_tiles - 2, 0)
    values(n_tiles - 1, 1)
    o_ref[0] = acc_ref[...].astype(o_ref.dtype)


def _decode_attn_body(p, last_p, bias_ref, q_ref, kn_ref, vn_ref, u_ref, k_refs, v_refs,
                      o_ref, acc_ref, carry_ref):
    neg_from_here = u_ref[...]
    bias = bias_ref[...]
    head_rows = [pl.ds(h * HEAD_DIM, HEAD_DIM) for h in range(N_HEADS)]

    def visit(pages, valid):
        z_rows = [[None] * N_HEADS for _ in pages]
        for h, rows in enumerate(head_rows):
            qh = q_ref[0, rows, :]
            for i, (k_ref, _) in enumerate(pages):
                z_rows[i][h] = jnp.sum(k_ref[0, rows, :] * qh, axis=0, keepdims=True)
        scored = []
        for rows_of_page in z_rows:
            z = jnp.concatenate(rows_of_page, axis=0) + bias
            sp = _softplus2(z)
            if valid is not None:
                sp = jnp.where(valid, sp, 0.0)
                z = jnp.where(valid, z, MASKED_LOG)
            scored.append((z, sp.astype(BF16)))
        sums = [_dot(spb, neg_from_here) for _, spb in scored]
        c = carry_ref[...]
        weights = []
        for (z, _), from_here in zip(scored, sums):
            weights.append(jnp.exp2(z + from_here + c))
            c = c + from_here[:, 0:1]
        carry_ref[...] = c
        for h, rows in enumerate(head_rows):
            part = acc_ref[rows, :]
            for (_, v_ref), a in zip(pages, weights):
                part = part + a[h:h + 1, :] * v_ref[0, rows, :]
            acc_ref[rows, :] = part

    @pl.when(p == 0)
    def _():
        acc_ref[...] = jnp.zeros_like(acc_ref)
        carry_ref[...] = jnp.zeros_like(carry_ref)
        n_new = 1
        q_index = 0
        key_pos = lax.broadcasted_iota(jnp.int32, (N_HEADS, PAGE_SIZE), 1)
        visit([(kn_ref, vn_ref)], (key_pos < n_new) & (key_pos < q_index))

    visit(list(zip(k_refs, v_refs)), None)

    @pl.when(p == last_p)
    def _():
        ones = jnp.ones((8, PAGE_SIZE), F32)
        total = lax.dot_general(ones, acc_ref[...], (((1,), (1,)), ((), ())),
                                preferred_element_type=F32,
                                precision=lax.Precision.HIGHEST)
        o_ref[0] = total[0:1, :]


def _attention_kernel(pt_ref, bias_ref, q_ref, k_ref, v_ref, u_ref,
                      dbias_ref, dq_ref, kn_ref, vn_ref, du_ref, *rest, steps_per_seq):
    npg = PAGES_PER_STEP
    k_pages = rest[0:npg]
    v_pages = rest[npg:2 * npg]
    o_ref, do_ref = rest[2 * npg:2 * npg + 2]
    (acc_ref, carry_ref, zs_ref, sp_ref, a_ref, vh_ref,
     dacc_ref, dcarry_ref) = rest[2 * npg + 2:]
    step = ((pl.program_id(0) * pl.num_programs(1) + pl.program_id(1)) * pl.num_programs(2)
            + pl.program_id(2))
    _decode_attn_body(step % steps_per_seq, steps_per_seq - 1, dbias_ref, dq_ref, kn_ref,
                      vn_ref, du_ref, k_pages, v_pages, do_ref, dacc_ref, dcarry_ref)
    _prompt_attn_body(bias_ref, q_ref, k_ref, v_ref, u_ref, o_ref,
                      acc_ref, carry_ref, zs_ref, sp_ref, a_ref, vh_ref)


def _attention(q, k, v, bias2, u_tile, tq, dq, k_new, v_new, cache_k, cache_v, page_table,
               u_page):
    b, t, d = q.shape
    tk = u_tile.shape[0]
    n_pairs = d // LANES
    nq = t // tq
    s = dq.shape[0]
    n_pages = page_table.shape[1]
    npg = PAGES_PER_STEP
    steps_per_seq = n_pages // npg
    assert n_pages % npg == 0 and s * steps_per_seq == b * n_pairs * nq, (
        "the decode sweep is spread over exactly the prompt sweep's grid steps")
    pt_flat = page_table.reshape(-1)

    def linear(bi, hp, qi):
        return (bi * n_pairs + hp) * nq + qi

    def seq_map(bi, hp, qi, pt):
        return (linear(bi, hp, qi) // steps_per_seq, 0, 0)

    def page_spec(i):
        def index_map(bi, hp, qi, pt):
            step = linear(bi, hp, qi)
            si, p = step // steps_per_seq, step % steps_per_seq
            return (pt[si * n_pages + (n_pages - 1 - (p * npg + i))], 0, 0)
        return pl.BlockSpec((1, d, PAGE_SIZE), index_map)

    fixed = lambda bi, hp, qi, pt: (0, 0)
    per_seq = pl.BlockSpec((1, 1, d), seq_map)
    new_page = pl.BlockSpec((1, d, PAGE_SIZE), seq_map)
    q_block = pl.BlockSpec((1, tq, LANES), lambda bi, hp, qi, pt: (bi, qi, hp))
    kv_block = pl.BlockSpec((1, t, LANES), lambda bi, hp, qi, pt: (bi, 0, hp))
    grid_spec = pltpu.PrefetchScalarGridSpec(
        num_scalar_prefetch=1,
        grid=(b, n_pairs, nq),
        in_specs=[
            pl.BlockSpec(memory_space=pltpu.SMEM),
            q_block, kv_block, kv_block,
            _resident(u_tile.shape, fixed),
            _resident((N_HEADS, 1), fixed),
            new_page, new_page, new_page,
            _resident(u_page.shape, fixed),
        ] + [page_spec(i) for i in range(npg)] + [page_spec(i) for i in range(npg)],
        out_specs=[q_block, per_seq],
        scratch_shapes=[
            pltpu.VMEM((tq, LANES), F32),
            pltpu.VMEM((2, tq, 1), F32),
            pltpu.VMEM((2, 2, tq, tk), F32),
            pltpu.VMEM((2, 2, tq, tk), BF16),
            pltpu.VMEM((2, 2, tq, tk), BF16),
            pltpu.VMEM((2, t, LANES), BF16),
            pltpu.VMEM((d, PAGE_SIZE), F32),
            pltpu.VMEM((N_HEADS, 1), F32),
        ],
    )
    return pl.pallas_call(
        functools.partial(_attention_kernel, steps_per_seq=steps_per_seq),
        grid_spec=grid_spec,
        out_shape=[jax.ShapeDtypeStruct((b, t, d), BF16),
                   jax.ShapeDtypeStruct((s, 1, d), F32)],
        compiler_params=_cparams(("arbitrary", "arbitrary", "arbitrary")),
    )(pt_flat, bias2, q, k, v, u_tile, bias2.reshape(N_HEADS, 1), dq, k_new, v_new, u_page,
      *([cache_k] * npg), *([cache_v] * npg))


def _ffn_kernel(o_ref, x_ref, gt0_ref, wo_ref, g_ref, sc_ref, sh_ref, gt1_ref,
                wg_ref, wu_ref, wd_ref, out_ref, *, ff_chunk):
    x1 = x_ref[...] + gt0_ref[0] * _dot(o_ref[...], wo_ref[...])
    h = _modnorm(x1, g_ref[...], sc_ref[0], sh_ref[0]).astype(BF16)
    d_ff = wg_ref.shape[1]
    f = None
    for c0 in range(0, d_ff, ff_chunk):
        hg = _dot(h, wg_ref[:, c0:c0 + ff_chunk])
        hu = _dot(h, wu_ref[:, c0:c0 + ff_chunk])
        part = _dot((_silu(hg) * hu).astype(BF16), wd_ref[c0:c0 + ff_chunk, :])
        f = part if f is None else f + part
    out_ref[...] = x1 + gt1_ref[0] * f


def _attn_out_ffn(o, x, gate0, w_o_b, g, scale, shift, gate1, wg_b, wu_b, wd_b,
                  tm, rows_per_group):
    n, d = x.shape
    d_ff = wg_b.shape[1]
    row = pl.BlockSpec((tm, d), lambda i: (i, 0))
    fixed = lambda i: (0, 0)
    mod = lambda m: _mod_spec(m, tm, rows_per_group)
    return pl.pallas_call(
        functools.partial(_ffn_kernel, ff_chunk=d_ff // 2),
        grid=(n // tm,),
        in_specs=[
            row, row, mod(gate0), _resident((d, d), fixed),
            _resident((1, d), fixed), mod(scale), mod(shift), mod(gate1),
            _resident((d, d_ff), fixed), _resident((d, d_ff), fixed),
            _resident((d_ff, d), fixed),
        ],
        out_specs=row,
        out_shape=jax.ShapeDtypeStruct((n, d), F32),
        compiler_params=_cparams(("arbitrary",)),
    )(o, x, gate0, w_o_b, g, scale, shift, gate1, wg_b, wu_b, wd_b)


def _sgu_uv(x_ref, g_ref, sc_ref, sh_ref, win_ref, gv_ref):
    width = gv_ref.shape[1]
    h = _modnorm(x_ref[...], g_ref[...], sc_ref[0], sh_ref[0]).astype(BF16)
    u = _gelu_tanh(_dot(h, win_ref[:, 0:width]))
    v = _rms(_gelu_tanh(_dot(h, win_ref[:, width:2 * width])), gv_ref[...])
    return u, v


def _sgu_prompt_kernel(x_ref, g_ref, sc_ref, sh_ref, gt_ref, win_ref, gv_ref,
                       ws_ref, bs_ref, wout_ref, out_ref, mixed_ref):
    tm = x_ref.shape[0]
    n_groups = ws_ref.shape[0]
    u, v = _sgu_uv(x_ref, g_ref, sc_ref, sh_ref, win_ref, gv_ref)
    vb = v.astype(BF16)
    i_pos = lax.broadcasted_iota(jnp.int32, (CHUNK, CHUNK), 0)
    j_pos = lax.broadcasted_iota(jnp.int32, (CHUNK, CHUNK), 1)
    causal = j_pos <= i_pos
    for gi in range(n_groups):
        ws = jnp.where(causal, ws_ref[gi], 0.0).astype(BF16)
        c0 = gi * SGU_GROUP_DIM
        for ci in range(tm // CHUNK):
            r0 = ci * CHUNK
            mixed_ref[r0:r0 + CHUNK, c0:c0 + SGU_GROUP_DIM] = _dot(
                ws, vb[r0:r0 + CHUNK, c0:c0 + SGU_GROUP_DIM])
    bias = bs_ref[...]
    for ci in range(tm // CHUNK):
        r0 = ci * CHUNK
        mixed_ref[r0:r0 + CHUNK, :] += bias
    y = _dot((u * mixed_ref[...]).astype(BF16), wout_ref[...])
    out_ref[...] = x_ref[...] + gt_ref[0] * y


def _sgu_prompt(x, g, scale, shift, gate, win_b, g_v, w_s, bias_rows, wout_b,
                tm, rows_per_group):
    n, d = x.shape
    width = g_v.shape[1]
    row = pl.BlockSpec((tm, d), lambda i: (i, 0))
    fixed = lambda i: (0, 0)
    mod = lambda m: _mod_spec(m, tm, rows_per_group)
    return pl.pallas_call(
        _sgu_prompt_kernel,
        grid=(n // tm,),
        in_specs=[
            row, _resident((1, d), fixed), mod(scale), mod(shift), mod(gate),
            _resident((d, 2 * width), fixed), _resident((1, width), fixed),
            _resident(w_s.shape, lambda i: (0, 0, 0)),
            _resident(bias_rows.shape, fixed), _resident((width, d), fixed),
        ],
        out_specs=row,
        out_shape=jax.ShapeDtypeStruct((n, d), F32),
        scratch_shapes=[pltpu.VMEM((tm, width), F32)],
        compiler_params=_cparams(("arbitrary",)),
    )(x, g, scale, shift, gate, win_b, g_v, w_s, bias_rows, wout_b)


def _sgu_decode_kernel(x_ref, g_ref, sc_ref, sh_ref, gt_ref, win_ref, gv_ref,
                       ws0_ref, bs0_ref, wout_ref, out_ref, v_ref):
    u, v = _sgu_uv(x_ref, g_ref, sc_ref, sh_ref, win_ref, gv_ref)
    v_ref[...] = v
    mixed = ws0_ref[...] * v + bs0_ref[...]
    y = _dot((u * mixed).astype(BF16), wout_ref[...])
    out_ref[...] = x_ref[...] + gt_ref[0] * y


def _sgu_decode(x, g, scale, shift, gate, win_b, g_v, ws0_row, bs0_row, wout_b):
    n, d = x.shape
    width = g_v.shape[1]
    whole = lambda a: pl.BlockSpec(a.shape, lambda i: (0,) * a.ndim)
    args = (x, g, scale, shift, gate, win_b, g_v, ws0_row, bs0_row, wout_b)
    return pl.pallas_call(
        _sgu_decode_kernel,
        grid=(1,),
        in_specs=[whole(a) for a in args],
        out_specs=[pl.BlockSpec((n, d), lambda i: (0, 0)),
                   pl.BlockSpec((n, width), lambda i: (0, 0))],
        out_shape=[jax.ShapeDtypeStruct((n, d), F32),
                   jax.ShapeDtypeStruct((n, width), F32)],
        compiler_params=_cparams(("arbitrary",)),
    )(*args)


def _route_kernel(x_ref, g_ref, sc_ref, sh_ref, wr_ref, h_ref, info_ref):
    h = _modnorm(x_ref[...], g_ref[...], sc_ref[0], sh_ref[0])
    h_ref[...] = h.reshape(h_ref.shape)
    logits = jnp.dot(h, wr_ref[...], preferred_element_type=F32,
                     precision=lax.Precision.HIGHEST)
    lane = lax.broadcasted_iota(jnp.int32, logits.shape, 1)
    lane_f = lane.astype(F32)
    neg = jnp.float32(-jnp.inf)
    l1 = jnp.where(lane < N_EXPERTS, logits, neg)
    m1 = jnp.max(l1, axis=-1, keepdims=True)
    i1 = jnp.min(jnp.where(l1 == m1, lane_f, float(LANES)), axis=-1, keepdims=True)
    l2 = jnp.where(lane_f == i1, neg, l1)
    m2 = jnp.max(l2, axis=-1, keepdims=True)
    i2 = jnp.min(jnp.where(l2 == m2, lane_f, float(LANES)), axis=-1, keepdims=True)
    e = jnp.exp(m2 - m1)
    g1 = 1.0 / (1.0 + e)
    g2 = e / (1.0 + e)
    info = jnp.where(lane == 0, g1, 0.0)
    info = jnp.where(lane == 1, g2, info)
    info = jnp.where(lane == 2, i1, info)
    info = jnp.where(lane == 3, i2, info)
    info_ref[...] = info


def _route(x, g, scale, shift, w_router_pad, tm, rows_per_group):
    n, d = x.shape
    assert d == ROW_TILE[0] * ROW_TILE[1]
    row = pl.BlockSpec((tm, d), lambda i: (i, 0))
    fixed = lambda i: (0, 0)
    return pl.pallas_call(
        _route_kernel,
        grid=(n // tm,),
        in_specs=[
            row, _resident((1, d), fixed),
            _mod_spec(scale, tm, rows_per_group), _mod_spec(shift, tm, rows_per_group),
            _resident((d, LANES), fixed),
        ],
        out_specs=[pl.BlockSpec((tm,) + ROW_TILE, lambda i: (i, 0, 0)),
                   pl.BlockSpec((tm, LANES), lambda i: (i, 0))],
        out_shape=[jax.ShapeDtypeStruct((n,) + ROW_TILE, F32),
                   jax.ShapeDtypeStruct((n, LANES), F32)],
        compiler_params=_cparams(("arbitrary",)),
    )(x, g, scale, shift, w_router_pad)


def _scatter_rows_kernel(pos_ref, src_ref, *rest, tm, zero_fill):
    if zero_fill:
        out_ref, zero_ref, sem = rest
    else:
        _, out_ref, sem = rest
    i = pl.program_id(0)
    n_copies = TOP_K * tm

    if zero_fill:
        @pl.when(i == 0)
        def _():
            fill = zero_ref.shape[0]
            zero_ref[...] = jnp.zeros_like(zero_ref)

            def fill_copy(j):
                return pltpu.make_async_copy(
                    zero_ref, out_ref.at[pl.ds(pl.multiple_of(j * fill, fill), fill)], sem)

            def start(j, c):
                fill_copy(j).start()
                return c

            def wait(j, c):
                fill_copy(j).wait()
                return c

            lax.fori_loop(0, out_ref.shape[0] // fill, start, 0)
            lax.fori_loop(0, out_ref.shape[0] // fill, wait, 0)

    def issue(r, c):
        for k in range(TOP_K):
            pltpu.make_async_copy(src_ref.at[pl.ds(r, 1)],
                                  out_ref.at[pl.ds(pos_ref[0, 0, TOP_K * r + k], 1)],
                                  sem).start()
        return c

    lax.fori_loop(0, tm, issue, 0, unroll=8)
    pltpu.make_async_copy(out_ref.at[pl.ds(0, n_copies)],
                          out_ref.at[pl.ds(n_copies, n_copies)], sem).wait()


def _scatter_rows(src, pos, tm, n_out=None, dst=None):
    n = src.shape[0]
    n_tiles = n // tm
    zero_fill = dst is None
    n_out = n_out if zero_fill else dst.shape[0]
    any_spec = pl.BlockSpec(memory_space=pl.ANY)
    pos_spec = pl.BlockSpec((1, 1, TOP_K * tm), lambda i: (i, 0, 0), memory_space=pltpu.SMEM)
    scratch = [pltpu.SemaphoreType.DMA(())]
    if zero_fill:
        scratch = [pltpu.VMEM((TM_MOE,) + ROW_TILE, src.dtype)] + scratch
    return pl.pallas_call(
        functools.partial(_scatter_rows_kernel, tm=tm, zero_fill=zero_fill),
        grid=(n_tiles,),
        in_specs=[pos_spec, pl.BlockSpec((tm,) + ROW_TILE, lambda i: (i, 0, 0))]
        + ([] if zero_fill else [any_spec]),
        out_specs=any_spec,
        out_shape=jax.ShapeDtypeStruct((n_out,) + ROW_TILE, src.dtype),
        scratch_shapes=scratch,
        input_output_aliases={} if zero_fill else {2: 0},
        compiler_params=_cparams(("arbitrary",)),
    )(pos.reshape(n_tiles, 1, TOP_K * tm), src, *([] if zero_fill else [dst]))


def _expert_kernel(te_ref, nu_ref, xs_ref, wg_ref, wu_ref, wd_ref, out_ref,
                   xb_ref, acc_ref):
    i = pl.program_id(0)
    c = pl.program_id(1)

    @pl.when(i < nu_ref[0])
    def _():
        @pl.when(c == 0)
        def _():
            xb_ref[...] = xs_ref[...].reshape(xb_ref.shape).astype(BF16)
            acc_ref[...] = jnp.zeros_like(acc_ref)

        xb = xb_ref[...]
        hg = _dot(xb, wg_ref[0])
        hu = _dot(xb, wu_ref[0])
        acc_ref[...] += _dot((_silu(hg) * hu).astype(BF16), wd_ref[0])

        @pl.when(c == pl.num_programs(1) - 1)
        def _():
            out_ref[...] = acc_ref[...].reshape(out_ref.shape)

    @pl.when((i >= nu_ref[0]) & (c == pl.num_programs(1) - 1))
    def _():
        out_ref[...] = jnp.zeros_like(out_ref)


def _expert_swiglu(xs, tile_expert, n_used, wg_b, wu_b, wd_b, tm, fc):
    n_rows = xs.shape[0]
    d = wg_b.shape[1]
    d_ff = wg_b.shape[2]
    n_tiles = n_rows // tm
    n_chunks = d_ff // fc

    def live(i, nu):
        return jnp.minimum(i, nu[0] - 1)

    def chunk(i, c, nu):
        return jnp.where(i < nu[0], c, n_chunks - 1)

    grid_spec = pltpu.PrefetchScalarGridSpec(
        num_scalar_prefetch=2,
        grid=(n_tiles, n_chunks),
        in_specs=[
            pl.BlockSpec((tm,) + ROW_TILE, lambda i, c, te, nu: (live(i, nu), 0, 0)),
            pl.BlockSpec((1, d, fc), lambda i, c, te, nu: (te[live(i, nu)], 0, chunk(i, c, nu))),
            pl.BlockSpec((1, d, fc), lambda i, c, te, nu: (te[live(i, nu)], 0, chunk(i, c, nu))),
            pl.BlockSpec((1, fc, d), lambda i, c, te, nu: (te[live(i, nu)], chunk(i, c, nu), 0)),
        ],
        out_specs=pl.BlockSpec((tm,) + ROW_TILE, lambda i, c, te, nu: (i, 0, 0)),
        scratch_shapes=[pltpu.VMEM((tm, d), BF16), pltpu.VMEM((tm, d), F32)],
    )
    return pl.pallas_call(
        _expert_kernel,
        grid_spec=grid_spec,
        out_shape=jax.ShapeDtypeStruct((n_rows,) + ROW_TILE, F32),
        compiler_params=_cparams(("arbitrary", "arbitrary")),
    )(tile_expert, n_used, xs, wg_b, wu_b, wd_b)


def _combine_kernel(pos_ref, x_ref, info_ref, gt_ref, gf_ref, y_ref, out_ref, ybuf, sem):
    tc = x_ref.shape[0]

    def issue(r, c):
        for k in range(TOP_K):
            pltpu.make_async_copy(y_ref.at[pl.ds(pos_ref[0, 0, TOP_K * r + k], 1)],
                                  ybuf.at[k, pl.ds(r, 1)], sem).start()
        return c

    lax.fori_loop(0, tc, issue, 0, unroll=min(8, tc))
    for k in range(TOP_K):
        pltpu.make_async_copy(y_ref.at[pl.ds(0, tc)], ybuf.at[k], sem).wait()
    info = info_ref[...]
    y0 = ybuf[0].reshape(x_ref.shape)
    y1 = ybuf[1].reshape(x_ref.shape)
    moe = info[:, 0:1] * y0 + info[:, 1:2] * y1
    x = x_ref[...] + gt_ref[0] * moe
    out_ref[...] = _rms(x, gf_ref[...])


def _combine(x, info, gate, final_g, y_sorted, pos, tc, rows_per_group):
    n, d = x.shape
    n_tiles = n // tc
    return pl.pallas_call(
        _combine_kernel,
        grid=(n_tiles,),
        in_specs=[
            pl.BlockSpec((1, 1, TOP_K * tc), lambda i: (i, 0, 0), memory_space=pltpu.SMEM),
            pl.BlockSpec((tc, d), lambda i: (i, 0)),
            pl.BlockSpec((tc, LANES), lambda i: (i, 0)),
            _mod_spec(gate, tc, rows_per_group),
            _resident((1, d), lambda i: (0, 0)),
            pl.BlockSpec(memory_space=pl.ANY),
        ],
        out_specs=pl.BlockSpec((tc, d), lambda i: (i, 0)),
        out_shape=jax.ShapeDtypeStruct((n, d), F32),
        scratch_shapes=[pltpu.VMEM((TOP_K, tc) + ROW_TILE, F32),
                        pltpu.SemaphoreType.DMA(())],
        compiler_params=_cparams(("arbitrary",)),
    )(pos.reshape(n_tiles, 1, TOP_K * tc), x, info, gate, final_g, y_sorted)


def _expert_layout(experts, tm):
    n_assign = experts.shape[0]
    n_tiles = n_assign // tm + N_EXPERTS
    onehot = (experts[:, None] == jnp.arange(N_EXPERTS, dtype=jnp.int32)[None, :]).astype(jnp.int32)
    running = jnp.cumsum(onehot, axis=0)
    counts = running[-1]
    rank = jnp.sum(running * onehot, axis=1) - 1
    tiles_per_expert = (counts + tm - 1) // tm
    tile_end = jnp.cumsum(tiles_per_expert)
    row_start = (tile_end - tiles_per_expert) * tm
    pos = row_start[experts] + rank
    n_used = tile_end[-1:]
    tile_ids = jnp.arange(n_tiles, dtype=jnp.int32)
    tile_expert = jnp.minimum(
        jnp.sum((tile_ids[:, None] >= tile_end[None, :]).astype(jnp.int32), axis=1),
        N_EXPERTS - 1).astype(jnp.int32)
    return pos.astype(jnp.int32), tile_expert, n_used.astype(jnp.int32), n_tiles


def kernel(x_prompt, x_sample, cache_k, cache_v, page_table, c_prompt, c_sample, norm_g, ada_w, ada_b, final_g, attn_w_qkv, attn_w_o, attn_b_score, sgu_w_in, sgu_g_v, sgu_w_s, sgu_b_s, sgu_w_out, ffn_w_gate, ffn_w_up, ffn_w_down, moe_w_router, moe_w_gate, moe_w_up, moe_w_down):
    b, t, d = x_prompt.shape
    s, t_new, _ = x_sample.shape
    assert t_new == 1, "the sample group decodes one token per sequence"
    n_p = b * t

    n_c = b + s
    n_c_pad = -(-n_c // 8) * 8
    c_all = jnp.concatenate(
        [c_prompt, c_sample, jnp.zeros((n_c_pad - n_c, d), F32)], axis=0)
    mods = _ada_modulation(c_all, ada_w, ada_b)

    def mod_parts(layer, sub):
        m = mods[2 * layer + sub]
        parts = []
        for j in range(3):
            col = m[:, j * d:(j + 1) * d]
            parts.append((col[0:b].reshape(b, 1, d), col[b:b + s].reshape(1, s, d)))
        return parts

    bf = lambda w: w.astype(BF16)
    g_row = lambda layer, sub: norm_g[layer, sub].reshape(1, d)
    xp = x_prompt.reshape(n_p, d)
    xs = x_sample.reshape(s, d)

    (sh_p, sh_s), (sc_p, sc_s), (gt0_p, gt0_s) = mod_parts(0, 0)
    w_qkv_b = bf(attn_w_qkv)
    q_p, kb_p, vb_p, k_p, v_p = _qkv(xp, g_row(0, 0), sc_p, sh_p, w_qkv_b, TM_QKV, t, True)
    q_s, kb_s, vb_s, k_s, v_s = _qkv(xs, g_row(0, 0), sc_s, sh_s, w_qkv_b, s, s, False)

    def neg_upper(n):
        i = jnp.arange(n, dtype=jnp.int32)
        return -(i[:, None] >= i[None, :]).astype(BF16)

    bias2 = attn_b_score * LOG2E
    keys_last = lambda c: c.transpose(0, 2, 3, 1).reshape(c.shape[0], d, PAGE_SIZE)

    def new_page(a):
        return jnp.pad(a[:, :, None], ((0, 0), (0, 0), (0, PAGE_SIZE - 1)))

    o_p, o_s = _attention(
        q_p.reshape(b, t, d), kb_p.reshape(b, t, d), vb_p.reshape(b, t, d), bias2,
        neg_upper(TK_ATTN), TQ_ATTN,
        jnp.broadcast_to(q_s.astype(F32)[:, :, None], (s, d, PAGE_SIZE)),
        new_page(k_s), new_page(v_s),
        keys_last(cache_k), keys_last(cache_v), page_table, neg_upper(PAGE_SIZE))
    o_s = o_s.astype(BF16)

    (sh_p, sh_s), (sc_p, sc_s), (gt1_p, gt1_s) = mod_parts(0, 1)
    w_o_b, wg_b, wu_b, wd_b = bf(attn_w_o), bf(ffn_w_gate), bf(ffn_w_up), bf(ffn_w_down)
    xp = _attn_out_ffn(o_p.reshape(n_p, d), xp, gt0_p, w_o_b, g_row(0, 1), sc_p, sh_p,
                       gt1_p, wg_b, wu_b, wd_b, TM_FFN, t)
    xs = _attn_out_ffn(o_s.reshape(s, d), xs, gt0_s, w_o_b, g_row(0, 1), sc_s, sh_s,
                       gt1_s, wg_b, wu_b, wd_b, s, s)

    (sh_p, sh_s), (sc_p, sc_s), (gt_p, gt_s) = mod_parts(1, 0)
    width = sgu_g_v.shape[0]
    n_groups = sgu_w_s.shape[0]
    win_b, wout_b = bf(sgu_w_in), bf(sgu_w_out)
    g_v = sgu_g_v.reshape(1, width)
    bias_rows = jnp.broadcast_to(sgu_b_s.T[:, :, None],
                                 (CHUNK, n_groups, SGU_GROUP_DIM)).reshape(CHUNK, width)
    xp = _sgu_prompt(xp, g_row(1, 0), sc_p, sh_p, gt_p, win_b, g_v, sgu_w_s,
                     bias_rows, wout_b, TM_SGU, t)
    ws0_row = jnp.broadcast_to(sgu_w_s[:, 0, 0][:, None],
                               (n_groups, SGU_GROUP_DIM)).reshape(1, width)
    xs, sgu_v_s = _sgu_decode(xs, g_row(1, 0), sc_s, sh_s, gt_s, win_b, g_v,
                              ws0_row, bias_rows[0:1], wout_b)

    (sh_p, sh_s), (sc_p, sc_s), (gt_p, gt_s) = mod_parts(1, 1)
    wr_pad = jnp.concatenate(
        [moe_w_router, jnp.zeros((d, LANES - N_EXPERTS), F32)], axis=1)
    h_p, info_p = _route(xp, g_row(1, 1), sc_p, sh_p, wr_pad, TM_ROUTE, t)
    h_s, info_s = _route(xs, g_row(1, 1), sc_s, sh_s, wr_pad, s, s)

    experts = jnp.concatenate([info_p[:, 2:2 + TOP_K], info_s[:, 2:2 + TOP_K]],
                              axis=0).astype(jnp.int32).reshape(-1)
    pos, tile_expert, n_used, n_tiles = _expert_layout(experts, TM_MOE)
    x_sorted = _scatter_rows(h_p, pos[:TOP_K * n_p], TM_MOE, n_out=n_tiles * TM_MOE)
    x_sorted = _scatter_rows(h_s, pos[TOP_K * n_p:], s, dst=x_sorted)
    y_sorted = _expert_swiglu(x_sorted, tile_expert, n_used, bf(moe_w_gate),
                              bf(moe_w_up), bf(moe_w_down), TM_MOE, FC_MOE)

    fg = final_g.reshape(1, d)
    y_p = _combine(xp, info_p, gt_p, fg, y_sorted, pos[:TOP_K * n_p], TC_COMBINE, t)
    y_s = _combine(xs, info_s, gt_s, fg, y_sorted, pos[TOP_K * n_p:], s, s)

    heads = (N_HEADS, HEAD_DIM)
    return (y_p.reshape(b, t, d), y_s.reshape(s, 1, d),
            k_p.reshape(b, *heads, t).transpose(0, 3, 1, 2),
            v_p.reshape(b, *heads, t).transpose(0, 3, 1, 2),
            k_s.reshape(s, 1, *heads), v_s.reshape(s, 1, *heads),
            sgu_v_s.reshape(s, 1, width))
```
